```python
import jax, jax.numpy as jnp
from jax import lax
import numpy as np

D_MODEL = 4096
BATCH = 16
SEQ = 256
DEPTH = 2
DEC_BATCH = 4
DEC_SEQ = 4096
PAST_LEN = 256

GRID_W = 64
EPS = 1e-6
N_EVEN = (DEPTH + 1) // 2
N_ODD = DEPTH // 2
A_W = D_MODEL // 2
B_W = D_MODEL // 2
C_W = D_MODEL // 2
D_W = D_MODEL // 2
HGRN_DK = 128
HGRN_H = A_W // HGRN_DK
HGRN_DV = A_W // HGRN_H
HGRN_CHUNK = 32
GMLP_CHUNK = 128
GMLP_CH = 128
GMLP_H = B_W // GMLP_CH
FNET_GROUPS = 4
FNET_CH = C_W // FNET_GROUPS
CONV_W = 31
CONV_PAD = (CONV_W - 1) // 2
EVEN_IN = 5 * A_W + 2 * B_W
EVEN_OUT = A_W + B_W
ODD_IN = C_W + 2 * D_W
ODD_OUT = C_W + D_W
PEER_HEADS = 8
PEER_DQ = 256
PEER_HALF = PEER_DQ // 2
PEER_NK = 128
PEER_N = PEER_NK * PEER_NK
PEER_TOPK = 16
PEER_BLOCK = 64

kernel_name = "hgrn2_gmlp_fnet_conformer_peer_dit_step"

F32 = jnp.float32


def rmsnorm(x, g):
    x32 = x.astype(F32)
    y = x32 * lax.rsqrt(jnp.mean(x32 * x32, axis=-1, keepdims=True) + EPS)
    return (y * g.astype(F32)).astype(x.dtype)


def to_chunks(t, c):
    b, l = t.shape[0], t.shape[1]
    return jnp.moveaxis(t.reshape((b, l // c, c) + t.shape[2:]), 1, 0)


def hgrn_scan(q, k, v, logf, s0):
    bsz, seqlen = q.shape[0], q.shape[1]
    c = HGRN_CHUNK
    mask = jnp.tril(jnp.ones((c, c), dtype=bool))

    def step(s, inp):
        qc, kc, vc, ac = inp
        b = jnp.cumsum(ac, axis=1)
        bm = b[:, c // 2:c // 2 + 1]
        o_inter = jnp.einsum('bchk,bhkv->bchv', qc * jnp.exp(b), s)
        qi = qc * jnp.exp(b - bm)
        ki = kc * jnp.exp(bm - b)
        a = jnp.einsum('bthk,bshk->bhts', qi, ki)
        a = jnp.where(mask, a, 0.0)
        o_intra = jnp.einsum('bhts,bshv->bthv', a, vc)
        bl = b[:, -1]
        k_end = kc * jnp.exp(bl[:, None] - b)
        s_new = jnp.exp(bl)[..., None] * s + jnp.einsum('bshk,bshv->bhkv', k_end, vc)
        return s_new, o_inter + o_intra

    s_fin, o = lax.scan(step, s0.astype(F32),
                        (to_chunks(q, c), to_chunks(k, c), to_chunks(v, c), to_chunks(logf, c)))
    o = jnp.moveaxis(o, 0, 1).reshape(bsz, seqlen, q.shape[2], v.shape[3])
    return o, s_fin


def even_mixer(h, l, j, s0_f, s0_b, even_w_in, hgrn_lb, hgrn_onorm_g, gmlp_vnorm_g, gmlp_ws, gmlp_bs, even_w_out):
    bsz, seqlen, _ = h.shape
    proj = h @ even_w_in[j]
    q, zf, zb, iv, g, u, v = jnp.split(
        proj, [A_W, 2 * A_W, 3 * A_W, 4 * A_W, 5 * A_W, 5 * A_W + B_W], axis=-1)
    hk = (bsz, seqlen, HGRN_H, HGRN_DK)
    qh = jax.nn.silu(q.astype(F32)).reshape(hk)
    ih = iv.astype(F32).reshape(bsz, seqlen, HGRN_H, HGRN_DV)
    lb = jnp.cumsum(jax.nn.softmax(hgrn_lb.astype(F32), axis=0), axis=0)[l]

    def gates(z, lbd):
        z = z.astype(F32).reshape(hk)
        logf = jnp.log(lbd + (1.0 - lbd) * jax.nn.sigmoid(z))
        return logf, (1.0 - lbd) * jax.nn.sigmoid(-z)

    a_f, k_f = gates(zf, lb[0])
    a_b, k_b = gates(zb, lb[1])
    o_f, s_f = hgrn_scan(qh, k_f, ih, a_f, s0_f)
    flip = lambda t: jnp.flip(t, axis=1)
    o_b, s_b = hgrn_scan(flip(qh), flip(k_b), flip(ih), flip(a_b), s0_b)
    o = o_f + flip(o_b)
    o = o * lax.rsqrt(jnp.mean(o * o, axis=-1, keepdims=True) + EPS) * hgrn_onorm_g[j].astype(F32)
    out_a = (o.reshape(bsz, seqlen, A_W) * jax.nn.silu(g.astype(F32))).astype(h.dtype)
    vv = v.astype(F32).reshape(bsz, seqlen // GMLP_CHUNK, GMLP_CHUNK, GMLP_H, GMLP_CH)
    vv = vv * lax.rsqrt(jnp.mean(vv * vv, axis=-1, keepdims=True) + EPS) * gmlp_vnorm_g[j].astype(F32)
    sp = jnp.einsum('hpq,bnqhc->bnphc', gmlp_ws[j].astype(F32), vv) \
        + gmlp_bs[j].astype(F32).T[None, None, :, :, None]
    out_b = (u.astype(F32) * sp.reshape(bsz, seqlen, B_W)).astype(h.dtype)
    out = jnp.concatenate([out_a, out_b], axis=-1) @ even_w_out[j]
    return out, s_f, s_b


def odd_mixer(h, j, latent, odd_w_in, conv_w, conv_b, conv_ln_g, conv_ln_b, odd_w_out):
    bsz, seqlen, _ = h.shape
    proj = h @ odd_w_in[j]
    xc, a, gt = jnp.split(proj, [C_W, C_W + D_W], axis=-1)
    fc = jnp.fft.fft2(xc.astype(F32).reshape(bsz, seqlen, FNET_GROUPS, FNET_CH), axes=(1, 3), norm='ortho').real
    out_c = fc.reshape(bsz, seqlen, C_W).astype(h.dtype)
    glu = a * jax.nn.sigmoid(gt)
    if latent:
        rows = seqlen // GRID_W
        seqs = glu.reshape(bsz * rows, GRID_W, D_W)
    else:
        seqs = glu
    y = lax.conv_general_dilated(seqs, conv_w[j][:, None, :], (1,), [(CONV_PAD, CONV_PAD)],
                                 dimension_numbers=('NWC', 'WIO', 'NWC'), feature_group_count=D_W)
    y = y.reshape(bsz, seqlen, D_W).astype(F32) + conv_b[j].astype(F32)
    mu = jnp.mean(y, axis=-1, keepdims=True)
    yc = y - mu
    y = yc * lax.rsqrt(jnp.mean(yc * yc, axis=-1, keepdims=True) + EPS) * conv_ln_g[j].astype(F32) \
        + conv_ln_b[j].astype(F32)
    out_d = jax.nn.silu(y).astype(h.dtype)
    return jnp.concatenate([out_c, out_d], axis=-1) @ odd_w_out[j]


def peer_ffn(h, wq, k1, k2, u_tab, v_tab):
    bsz, seqlen, d = h.shape
    t = bsz * seqlen
    x = h.reshape(t, d)
    q = (x @ wq).astype(F32).reshape(t, PEER_HEADS, PEER_DQ)
    s1 = jnp.einsum('thd,hnd->thn', q[..., :PEER_HALF], k1.astype(F32))
    s2 = jnp.einsum('thd,hnd->thn', q[..., PEER_HALF:], k2.astype(F32))
    v1, i1 = lax.top_k(s1, PEER_TOPK)
    v2, i2 = lax.top_k(s2, PEER_TOPK)
    cand = (v1[..., :, None] + v2[..., None, :]).reshape(t, PEER_HEADS, PEER_TOPK * PEER_TOPK)
    cidx = (i1[..., :, None] * PEER_NK + i2[..., None, :]).reshape(t, PEER_HEADS, PEER_TOPK * PEER_TOPK)
    sc, pos = lax.top_k(cand, PEER_TOPK)
    idx = jnp.take_along_axis(cidx, pos, axis=-1)
    gate = jax.nn.softmax(sc, axis=-1)
    nb = t // PEER_BLOCK

    def block(args):
        xb, ib, gb = args
        u = jnp.take(u_tab, ib, axis=0)
        act = jax.nn.gelu(jnp.einsum('td,thkd->thk', xb, u).astype(F32), approximate=False)
        w = (gb * act).astype(v_tab.dtype)
        vv = jnp.take(v_tab, ib, axis=0)
        return jnp.einsum('thk,thkd->td', w, vv)

    out = lax.map(block, (x.reshape(nb, PEER_BLOCK, d),
                          idx.reshape(nb, PEER_BLOCK, PEER_HEADS, PEER_TOPK),
                          gate.reshape(nb, PEER_BLOCK, PEER_HEADS, PEER_TOPK)))
    return out.reshape(bsz, seqlen, d).astype(h.dtype)


def run_trunk(x, cond, init_state, latent, norm_g, ada_w, ada_b, even_w_in, hgrn_lb, hgrn_onorm_g,
              gmlp_vnorm_g, gmlp_ws, gmlp_bs, even_w_out, odd_w_in, conv_w, conv_b, conv_ln_g, conv_ln_b,
              odd_w_out, peer_wq, peer_k1, peer_k2, peer_u, peer_v, final_g):
    bsz = x.shape[0]
    states = []
    for l in range(DEPTH):
        mod = (jax.nn.silu(cond) @ ada_w[l] + ada_b[l]).reshape(cond.shape[0], 6, 1, D_MODEL)
        h = rmsnorm(x, norm_g[l, 0]) * (1.0 + mod[:, 1]) + mod[:, 0]
        j = l // 2
        if l % 2 == 0:
            if init_state is None:
                s0_f = jnp.zeros((bsz, HGRN_H, HGRN_DK, HGRN_DV), F32)
                s0_b = s0_f
            else:
                s0_f = init_state[:, j, 0]
                s0_b = init_state[:, j, 1]
            out, s_f, s_b = even_mixer(h, l, j, s0_f, s0_b, even_w_in, hgrn_lb, hgrn_onorm_g,
                                       gmlp_vnorm_g, gmlp_ws, gmlp_bs, even_w_out)
            states.append(jnp.stack([s_f, s_b], axis=1))
        else:
            out = odd_mixer(h, j, latent, odd_w_in, conv_w, conv_b, conv_ln_g, conv_ln_b, odd_w_out)
        x = x + mod[:, 2] * out
        h = rmsnorm(x, norm_g[l, 1]) * (1.0 + mod[:, 4]) + mod[:, 3]
        x = x + mod[:, 5] * peer_ffn(h, peer_wq[l], peer_k1[l], peer_k2[l], peer_u[l], peer_v[l])
    return rmsnorm(x, final_g), states


def setup_inputs(seed: int = 0) -> dict:
    key = jax.random.key(seed)
    ks = iter(jax.random.split(key, 32))
    nrm = lambda shape, s: jax.random.normal(next(ks), shape, F32) * s
    return {
        "x_prompt": nrm((BATCH, SEQ, D_MODEL), 1.0),
        "x_sample": nrm((DEC_BATCH, DEC_SEQ, D_MODEL), 1.0),
        "state_hgrn": nrm((DEC_BATCH, N_EVEN, 2, HGRN_H, HGRN_DK, HGRN_DV), 0.5),
        "c": nrm((DEC_BATCH, D_MODEL), 1.0),
        "c_ctx": nrm((D_MODEL,), 1.0),
        "norm_g": 1.0 + nrm((DEPTH, 2, D_MODEL), 0.02),
        "ada_w": nrm((DEPTH, D_MODEL, 6 * D_MODEL), 0.5 * D_MODEL ** -0.5),
        "ada_b": nrm((DEPTH, 6 * D_MODEL), 0.01),
        "even_w_in": nrm((N_EVEN, D_MODEL, EVEN_IN), D_MODEL ** -0.5),
        "hgrn_lb": nrm((DEPTH + 1, 2, HGRN_H, HGRN_DK), 0.1),
        "hgrn_onorm_g": 1.0 + nrm((N_EVEN, HGRN_DV), 0.02),
        "gmlp_vnorm_g": 1.0 + nrm((N_EVEN, GMLP_H, GMLP_CH), 0.02),
        "gmlp_ws": nrm((N_EVEN, GMLP_H, GMLP_CHUNK, GMLP_CHUNK), GMLP_CHUNK ** -0.5),
        "gmlp_bs": 1.0 + nrm((N_EVEN, GMLP_H, GMLP_CHUNK), 0.1),
        "even_w_out": nrm((N_EVEN, EVEN_OUT, D_MODEL), EVEN_OUT ** -0.5),
        "odd_w_in": nrm((N_ODD, D_MODEL, ODD_IN), D_MODEL ** -0.5),
        "conv_w": nrm((N_ODD, CONV_W, D_W), CONV_W ** -0.5),
        "conv_b": nrm((N_ODD, D_W), 0.01),
        "conv_ln_g": 1.0 + nrm((N_ODD, D_W), 0.02),
        "conv_ln_b": nrm((N_ODD, D_W), 0.01),
        "odd_w_out": nrm((N_ODD, ODD_OUT, D_MODEL), ODD_OUT ** -0.5),
        "peer_wq": nrm((DEPTH, D_MODEL, PEER_HEADS * PEER_DQ), D_MODEL ** -0.5),
        "peer_k1": nrm((DEPTH, PEER_HEADS, PEER_NK, PEER_HALF), PEER_HALF ** -0.5),
        "peer_k2": nrm((DEPTH, PEER_HEADS, PEER_NK, PEER_HALF), PEER_HALF ** -0.5),
        "peer_u": nrm((DEPTH, PEER_N, D_MODEL), D_MODEL ** -0.5),
        "peer_v": nrm((DEPTH, PEER_N, D_MODEL), 1.0),
        "final_g": 1.0 + nrm((D_MODEL,), 0.02),
    }


def reference(x_prompt, x_sample, state_hgrn, c, c_ctx, norm_g, ada_w, ada_b, even_w_in, hgrn_lb,
              hgrn_onorm_g, gmlp_vnorm_g, gmlp_ws, gmlp_bs, even_w_out, odd_w_in, conv_w, conv_b,
              conv_ln_g, conv_ln_b, odd_w_out, peer_wq, peer_k1, peer_k2, peer_u, peer_v, final_g):
    y_prompt, ctx_states = run_trunk(
        x_prompt, c_ctx[None, :], None, False, norm_g, ada_w, ada_b, even_w_in, hgrn_lb, hgrn_onorm_g,
        gmlp_vnorm_g, gmlp_ws, gmlp_bs, even_w_out, odd_w_in, conv_w, conv_b, conv_ln_g, conv_ln_b,
        odd_w_out, peer_wq, peer_k1, peer_k2, peer_u, peer_v, final_g)
    new_state_hgrn = jnp.stack(ctx_states, axis=1)
    y_sample, _ = run_trunk(
        x_sample, c, state_hgrn, True, norm_g, ada_w, ada_b, even_w_in, hgrn_lb, hgrn_onorm_g,
        gmlp_vnorm_g, gmlp_ws, gmlp_bs, even_w_out, odd_w_in, conv_w, conv_b, conv_ln_g, conv_ln_b,
        odd_w_out, peer_wq, peer_k1, peer_k2, peer_u, peer_v, final_g)
    return (y_prompt, y_sample, new_state_hgrn)
```

```python
import functools

import jax
import jax.numpy as jnp
from jax import lax
from jax.experimental import pallas as pl
from jax.experimental.pallas import tpu as pltpu

F32 = jnp.float32
BF16 = jnp.bfloat16

D_MODEL = 4096
BATCH = 16
SEQ = 256
DEPTH = 2
DEC_BATCH = 4
DEC_SEQ = 4096
GRID_W = 64
EPS = 1e-6
A_W = D_MODEL // 2
B_W = D_MODEL // 2
C_W = D_MODEL // 2
D_W = D_MODEL // 2
HGRN_DK = 128
HGRN_H = A_W // HGRN_DK
HGRN_DV = A_W // HGRN_H
HGRN_CHUNK = 32
GMLP_CHUNK = 128
GMLP_CH = 128
GMLP_H = B_W // GMLP_CH
FNET_GROUPS = 4
FNET_CH = C_W // FNET_GROUPS
CONV_W = 31
CONV_PAD = (CONV_W - 1) // 2
PEER_HEADS = 8
PEER_DQ = 256
PEER_HALF = PEER_DQ // 2
PEER_NK = 128
PEER_TOPK = 16

SEG = 4096
NSEG = 1 + DEC_BATCH
TOKENS = NSEG * SEG
MOD_ROWS = 8

VMEM_LIMIT = 56 * 1024 * 1024
INV_SQRT2 = 0.7071067811865476


def _params(sem):
    return pltpu.CompilerParams(dimension_semantics=sem, vmem_limit_bytes=VMEM_LIMIT)


def _ada_kernel(c_ref, w_ref, b_ref, o_ref):
    c = c_ref[...]
    a = (c * jax.nn.sigmoid(c)).astype(BF16)
    o_ref[...] = jnp.dot(a, w_ref[...].astype(BF16), preferred_element_type=F32) + b_ref[...]


def ada_modulation(cond8, ada_w, ada_b, tn=512):
    depth, d, n = ada_w.shape
    return pl.pallas_call(
        _ada_kernel,
        grid=(depth, n // tn),
        in_specs=[
            pl.BlockSpec((8, d), lambda l, j: (0, 0)),
            pl.BlockSpec((None, d, tn), lambda l, j: (l, 0, j)),
            pl.BlockSpec((None, 1, tn), lambda l, j: (l, 0, j)),
        ],
        out_specs=pl.BlockSpec((None, 8, tn), lambda l, j: (l, 0, j)),
        out_shape=jax.ShapeDtypeStruct((depth, 8, n), F32),
        compiler_params=_params(("parallel", "parallel")),
        name="ada_modulation",
    )(cond8, ada_w, ada_b.reshape(depth, 1, n))


def _norm_kernel(*refs, gate_row, scale_row, shift_row, emit_x):
    it = iter(refs)
    x_ref = next(it)
    x = x_ref[...]
    if gate_row is not None:
        d_ref = next(it)
        pm_ref = next(it)
        x = x + pm_ref[gate_row:gate_row + 1, :] * d_ref[...]
    g_ref = next(it)
    m_ref = next(it) if scale_row is not None else None
    xo_ref = next(it) if emit_x else None
    h_ref = next(it)
    if emit_x:
        xo_ref[...] = x
    ms = jnp.mean(x * x, axis=-1, keepdims=True)
    y = x * lax.rsqrt(ms + EPS) * g_ref[...]
    if scale_row is not None:
        y = y * (1.0 + m_ref[scale_row:scale_row + 1, :]) + m_ref[shift_row:shift_row + 1, :]
    h_ref[...] = y.astype(h_ref.dtype)


def norm_modulate(x, g, *, delta=None, prev_mod=None, gate_row=None, mod=None, scale_row=None,
                  shift_row=None, emit_x=False, out_dtype=BF16, tr=256):
    t, d = x.shape
    row = pl.BlockSpec((tr, d), lambda i: (i, 0))
    modspec = pl.BlockSpec((None, MOD_ROWS, d), lambda i: ((i * tr) // SEG, 0, 0))
    args, specs = [x], [row]
    if delta is not None:
        args += [delta, prev_mod]
        specs += [row, modspec]
    args.append(g.reshape(1, d))
    specs.append(pl.BlockSpec((1, d), lambda i: (0, 0)))
    if mod is not None:
        args.append(mod)
        specs.append(modspec)
    out_shape = [jax.ShapeDtypeStruct((t, d), out_dtype)]
    out_specs = [row]
    if emit_x:
        out_shape.insert(0, jax.ShapeDtypeStruct((t, d), F32))
        out_specs.insert(0, row)
    kern = functools.partial(_norm_kernel, gate_row=gate_row if delta is not None else None,
                             scale_row=scale_row if mod is not None else None,
                             shift_row=shift_row, emit_x=emit_x)
    out = pl.pallas_call(
        kern, grid=(t // tr,), in_specs=specs, out_specs=out_specs, out_shape=out_shape,
        compiler_params=_params(("parallel",)), name="norm_modulate",
    )(*args)
    return out if emit_x else out[0]


def _mm_kernel(*refs, gate_row):
    if gate_row is None:
        x_ref, w_ref, o_ref = refs
    else:
        x_ref, w_ref, r_ref, m_ref, o_ref = refs
    acc = jnp.dot(x_ref[...], w_ref[...], preferred_element_type=F32)
    if gate_row is not None:
        acc = r_ref[...] + m_ref[gate_row:gate_row + 1, :] * acc
    o_ref[...] = acc.astype(o_ref.dtype)


def matmul(x, w, *, out_dtype=F32, res=None, mod=None, gate_row=None, tm=1024, tn=512):
    m, k = x.shape
    n = w.shape[1]
    tm, tn = min(tm, m), min(tn, n)
    specs = [pl.BlockSpec((tm, k), lambda i, j: (i, 0)), pl.BlockSpec((k, tn), lambda i, j: (0, j))]
    args = [x, w]
    if res is not None:
        specs += [pl.BlockSpec((tm, tn), lambda i, j: (i, j)),
                  pl.BlockSpec((None, MOD_ROWS, tn), lambda i, j: ((i * tm) // SEG, 0, j))]
        args += [res, mod]
    return pl.pallas_call(
        functools.partial(_mm_kernel, gate_row=gate_row if res is not None else None),
        grid=(m // tm, n // tn), in_specs=specs,
        out_specs=pl.BlockSpec((tm, tn), lambda i, j: (i, j)),
        out_shape=jax.ShapeDtypeStruct((m, n), out_dtype),
        compiler_params=_params(("parallel", "arbitrary")), name="matmul",
    )(*args)


def _peer_kernel(x_ref, ut_ref, v_ref, s1_ref, e1_ref, s2_ref, e2_ref, tau_ref, o_ref, *, n_a):
    @pl.when(pl.program_id(1) == 0)
    def _():
        o_ref[...] = jnp.zeros_like(o_ref)

    s = jnp.dot(x_ref[...], ut_ref[...], preferred_element_type=F32)
    act = 0.5 * s * (1.0 + lax.erf(s * INV_SQRT2))
    gates = []
    for al in range(n_a):
        g = None
        for h in range(PEER_HEADS):
            col = al * PEER_HEADS + h
            lanes = slice(h * PEER_NK, (h + 1) * PEER_NK)
            score = s1_ref[:, col:col + 1] + s2_ref[:, lanes]
            term = jnp.where(score >= tau_ref[:, lanes], e1_ref[:, col:col + 1] * e2_ref[:, lanes], 0.0)
            g = term if g is None else g + term
        gates.append(g)
    w = (jnp.concatenate(gates, axis=1) * act).astype(BF16)
    o_ref[...] += jnp.dot(w, v_ref[...], preferred_element_type=F32)


def peer_dense(h, ut, v, s1r, e1r, s2, e2, tau, *, tm=512, n_a=4):
    t, d = h.shape
    tn = n_a * PEER_NK
    nj = ut.shape[1] // tn
    wide = PEER_HEADS * PEER_NK
    tok = lambda i, j: (i, 0)
    once = pl.Buffered(1)
    return pl.pallas_call(
        functools.partial(_peer_kernel, n_a=n_a),
        grid=(t // tm, nj),
        in_specs=[
            pl.BlockSpec((tm, d), tok, pipeline_mode=once),
            pl.BlockSpec((d, tn), lambda i, j: (0, j)),
            pl.BlockSpec((tn, d), lambda i, j: (j, 0)),
            pl.BlockSpec((None, tm, n_a * PEER_HEADS), lambda i, j: (j, i, 0)),
            pl.BlockSpec((None, tm, n_a * PEER_HEADS), lambda i, j: (j, i, 0)),
            pl.BlockSpec((tm, wide), tok, pipeline_mode=once),
            pl.BlockSpec((tm, wide), tok, pipeline_mode=once),
            pl.BlockSpec((tm, wide), tok, pipeline_mode=once),
        ],
        out_specs=pl.BlockSpec((tm, d), tok, pipeline_mode=once),
        out_shape=jax.ShapeDtypeStruct((t, d), F32),
        compiler_params=_params(("parallel", "arbitrary")), name="peer_dense",
    )(h, ut, v, s1r, e1r, s2, e2, tau)


def peer_routing(q, k1, k2, n_a):
    t = q.shape[0]
    q = q.reshape(t, PEER_HEADS, PEER_DQ)
    s1 = jnp.einsum('thd,hnd->thn', q[..., :PEER_HALF], k1)
    s2 = jnp.einsum('thd,hnd->thn', q[..., PEER_HALF:], k2)
    v1, _ = lax.top_k(s1, PEER_TOPK)
    v2, _ = lax.top_k(s2, PEER_TOPK)
    cand = (v1[..., :, None] + v2[..., None, :]).reshape(t, PEER_HEADS, PEER_TOPK * PEER_TOPK)
    sc, _ = lax.top_k(cand, PEER_TOPK)
    tau = sc[..., PEER_TOPK - 1:]
    z = jnp.sum(jnp.exp(sc - sc[..., :1]), axis=-1, keepdims=True)
    e1 = jnp.exp(s1 - v1[..., :1])
    e2 = jnp.exp(s2 - v2[..., :1]) / z
    wide = PEER_HEADS * PEER_NK
    slab = lambda a: a.reshape(t, PEER_HEADS, PEER_NK // n_a, n_a).transpose(2, 0, 3, 1).reshape(
        PEER_NK // n_a, t, n_a * PEER_HEADS)
    return (slab(s1), slab(e1), s2.reshape(t, wide), e2.reshape(t, wide),
            jnp.broadcast_to(tau, s2.shape).reshape(t, wide))


def _to_chunks(t, c):
    b, l = t.shape[0], t.shape[1]
    return jnp.moveaxis(t.reshape((b, l // c, c) + t.shape[2:]), 1, 0)


def _hgrn_scan(q, k, v, logf, s0):
    bsz, seqlen = q.shape[0], q.shape[1]
    c = HGRN_CHUNK
    mask = jnp.tril(jnp.ones((c, c), dtype=bool))

    def step(s, inp):
        qc, kc, vc, ac = inp
        b = jnp.cumsum(ac, axis=1)
        bm = b[:, c // 2:c // 2 + 1]
        o_inter = jnp.einsum('bchk,bhkv->bchv', qc * jnp.exp(b), s)
        a = jnp.einsum('bthk,bshk->bhts', qc * jnp.exp(b - bm), kc * jnp.exp(bm - b))
        a = jnp.where(mask, a, 0.0)
        o_intra = jnp.einsum('bhts,bshv->bthv', a, vc)
        bl = b[:, -1]
        k_end = kc * jnp.exp(bl[:, None] - b)
        s_new = jnp.exp(bl)[..., None] * s + jnp.einsum('bshk,bshv->bhkv', k_end, vc)
        return s_new, o_inter + o_intra

    s_fin, o = lax.scan(step, s0.astype(F32),
                        (_to_chunks(q, c), _to_chunks(k, c), _to_chunks(v, c), _to_chunks(logf, c)))
    o = jnp.moveaxis(o, 0, 1).reshape(bsz, seqlen, q.shape[2], v.shape[3])
    return o, s_fin


def _even_mix(proj, layer, j, s0_f, s0_b, hgrn_lb, hgrn_onorm_g, gmlp_vnorm_g, gmlp_ws, gmlp_bs):
    bsz, seqlen, _ = proj.shape
    q, zf, zb, iv, g, u, v = jnp.split(
        proj, [A_W, 2 * A_W, 3 * A_W, 4 * A_W, 5 * A_W, 5 * A_W + B_W], axis=-1)
    hk = (bsz, seqlen, HGRN_H, HGRN_DK)
    qh = jax.nn.silu(q).reshape(hk)
    ih = iv.reshape(bsz, seqlen, HGRN_H, HGRN_DV)
    lb = jnp.cumsum(jax.nn.softmax(hgrn_lb.astype(F32), axis=0), axis=0)[layer]

    def gates(z, lbd):
        z = z.reshape(hk)
        return jnp.log(lbd + (1.0 - lbd) * jax.nn.sigmoid(z)), (1.0 - lbd) * jax.nn.sigmoid(-z)

    a_f, k_f = gates(zf, lb[0])
    a_b, k_b = gates(zb, lb[1])
    o_f, s_f = _hgrn_scan(qh, k_f, ih, a_f, s0_f)
    flip = lambda t: jnp.flip(t, axis=1)
    o_b, s_b = _hgrn_scan(flip(qh), flip(k_b), flip(ih), flip(a_b), s0_b)
    o = o_f + flip(o_b)
    o = o * lax.rsqrt(jnp.mean(o * o, axis=-1, keepdims=True) + EPS) * hgrn_onorm_g[j]
    out_a = o.reshape(bsz, seqlen, A_W) * jax.nn.silu(g)
    vv = v.reshape(bsz, seqlen // GMLP_CHUNK, GMLP_CHUNK, GMLP_H, GMLP_CH)
    vv = vv * lax.rsqrt(jnp.mean(vv * vv, axis=-1, keepdims=True) + EPS) * gmlp_vnorm_g[j]
    sp = jnp.einsum('hpq,bnqhc->bnphc', gmlp_ws[j], vv) + gmlp_bs[j].T[None, None, :, :, None]
    out_b = u * sp.reshape(bsz, seqlen, B_W)
    return jnp.concatenate([out_a, out_b], axis=-1).astype(BF16), s_f, s_b


def _odd_mix(proj, j, latent, conv_w, conv_b, conv_ln_g, conv_ln_b):
    bsz, seqlen, _ = proj.shape
    xc, a, gt = jnp.split(proj, [C_W, C_W + D_W], axis=-1)
    fc = jnp.fft.fft2(xc.reshape(bsz, seqlen, FNET_GROUPS, FNET_CH), axes=(1, 3), norm='ortho').real
    out_c = fc.reshape(bsz, seqlen, C_W)
    glu = a * jax.nn.sigmoid(gt)
    seqs = glu.reshape(bsz * (seqlen // GRID_W), GRID_W, D_W) if latent else glu
    y = lax.conv_general_dilated(seqs, conv_w[j][:, None, :], (1,), [(CONV_PAD, CONV_PAD)],
                                 dimension_numbers=('NWC', 'WIO', 'NWC'), feature_group_count=D_W)
    y = y.reshape(bsz, seqlen, D_W) + conv_b[j]
    yc = y - jnp.mean(y, axis=-1, keepdims=True)
    y = yc * lax.rsqrt(jnp.mean(yc * yc, axis=-1, keepdims=True) + EPS) * conv_ln_g[j] + conv_ln_b[j]
    return jnp.concatenate([out_c, jax.nn.silu(y)], axis=-1).astype(BF16)


def kernel(x_prompt, x_sample, state_hgrn, c, c_ctx, norm_g, ada_w, ada_b, even_w_in, hgrn_lb, hgrn_onorm_g, gmlp_vnorm_g, gmlp_ws, gmlp_bs, even_w_out, odd_w_in, conv_w, conv_b, conv_ln_g, conv_ln_b, odd_w_out, peer_wq, peer_k1, peer_k2, peer_u, peer_v, final_g):
    n_a = 4
    n_ctx = BATCH * SEQ
    x = jnp.concatenate([x_prompt.reshape(n_ctx, D_MODEL), x_sample.reshape(DEC_BATCH * DEC_SEQ, D_MODEL)], axis=0)

    cond8 = jnp.zeros((8, D_MODEL), F32).at[0].set(c_ctx).at[1:1 + DEC_BATCH].set(c)
    mod_all = ada_modulation(cond8, ada_w, ada_b)
    mod_all = mod_all[:, :NSEG].reshape(DEPTH, NSEG, 6, D_MODEL)
    mod_all = jnp.pad(mod_all, ((0, 0), (0, 0), (0, MOD_ROWS - 6), (0, 0)))

    states = []
    delta = prev_mod = None
    for l in range(DEPTH):
        mod = mod_all[l]
        j = l // 2
        if delta is None:
            h = norm_modulate(x, norm_g[l, 0], mod=mod, scale_row=1, shift_row=0)
        else:
            x, h = norm_modulate(x, norm_g[l, 0], delta=delta, prev_mod=prev_mod, gate_row=5,
                                 mod=mod, scale_row=1, shift_row=0, emit_x=True)
        if l % 2 == 0:
            proj = matmul(h, even_w_in[j].astype(BF16))
            ew = (hgrn_lb, hgrn_onorm_g, gmlp_vnorm_g, gmlp_ws, gmlp_bs)
            zero = jnp.zeros((BATCH, HGRN_H, HGRN_DK, HGRN_DV), F32)
            mix_c, s_f, s_b = _even_mix(proj[:n_ctx].reshape(BATCH, SEQ, -1), l, j, zero, zero, *ew)
            states.append(jnp.stack([s_f, s_b], axis=1))
            mix_l, _, _ = _even_mix(proj[n_ctx:].reshape(DEC_BATCH, DEC_SEQ, -1), l, j,
                                    state_hgrn[:, j, 0], state_hgrn[:, j, 1], *ew)
            w_out = even_w_out[j]
        else:
            proj = matmul(h, odd_w_in[j].astype(BF16))
            ow = (conv_w, conv_b, conv_ln_g, conv_ln_b)
            mix_c = _odd_mix(proj[:n_ctx].reshape(BATCH, SEQ, -1), j, False, *ow)
            mix_l = _odd_mix(proj[n_ctx:].reshape(DEC_BATCH, DEC_SEQ, -1), j, True, *ow)
            w_out = odd_w_out[j]
        mix = jnp.concatenate([mix_c.reshape(n_ctx, D_MODEL), mix_l.reshape(-1, D_MODEL)], axis=0)
        x = matmul(mix, w_out.astype(BF16), res=x, mod=mod, gate_row=2)

        h2 = norm_modulate(x, norm_g[l, 1], mod=mod, scale_row=4, shift_row=3)
        q = matmul(h2, peer_wq[l].astype(BF16))
        routing = peer_routing(q, peer_k1[l], peer_k2[l], n_a)
        delta = peer_dense(h2, peer_u[l].T.astype(BF16), peer_v[l].astype(BF16), *routing, n_a=n_a)
        prev_mod = mod

    y = norm_modulate(x, final_g, delta=delta, prev_mod=prev_mod, gate_row=5, out_dtype=F32)
    y_prompt = y[:n_ctx].reshape(BATCH, SEQ, D_MODEL)
    y_sample = y[n_ctx:].reshape(DEC_BATCH, DEC_SEQ, D_MODEL)
    return (y_prompt, y_sample, jnp.stack(states, axis=1))
```

```python
import functools

import jax
import jax.numpy as jnp
from jax import lax
from jax.experimental import pallas as pl
from jax.experimental.pallas import tpu as pltpu

F32 = jnp.float32
BF16 = jnp.bfloat16

D_MODEL = 4096
BATCH = 16
SEQ = 256
DEPTH = 2
DEC_BATCH = 4
DEC_SEQ = 4096
GRID_W = 64
EPS = 1e-6
A_W = D_MODEL // 2
B_W = D_MODEL // 2
C_W = D_MODEL // 2
D_W = D_MODEL // 2
HGRN_DK = 128
HGRN_H = A_W // HGRN_DK
HGRN_DV = A_W // HGRN_H
HGRN_CHUNK = 32
GMLP_CHUNK = 128
GMLP_CH = 128
GMLP_H = B_W // GMLP_CH
FNET_GROUPS = 4
FNET_CH = C_W // FNET_GROUPS
CONV_W = 31
CONV_PAD = (CONV_W - 1) // 2
PEER_HEADS = 8
PEER_DQ = 256
PEER_HALF = PEER_DQ // 2
PEER_NK = 128
PEER_TOPK = 16

SEG = 4096
NSEG = 1 + DEC_BATCH
TOKENS = NSEG * SEG
MOD_ROWS = 8
LANES = 128
SUBLANES = 8

VMEM_LIMIT = 56 * 1024 * 1024
INV_SQRT2 = 0.7071067811865476
NEG_INF = float("-inf")
NO_RANK = 1e9


def _params(sem):
    return pltpu.CompilerParams(dimension_semantics=sem, vmem_limit_bytes=VMEM_LIMIT)


def _ada_kernel(c_ref, w_ref, b_ref, o_ref):
    c = c_ref[...]
    a = (c * jax.nn.sigmoid(c)).astype(BF16)
    o_ref[...] = jnp.dot(a, w_ref[...].astype(BF16), preferred_element_type=F32) + b_ref[...]


def ada_modulation(cond8, ada_w, ada_b, tn=512):
    depth, d, n = ada_w.shape
    return pl.pallas_call(
        _ada_kernel,
        grid=(depth, n // tn),
        in_specs=[
            pl.BlockSpec((8, d), lambda l, j: (0, 0)),
            pl.BlockSpec((None, d, tn), lambda l, j: (l, 0, j)),
            pl.BlockSpec((None, 1, tn), lambda l, j: (l, 0, j)),
        ],
        out_specs=pl.BlockSpec((None, 8, tn), lambda l, j: (l, 0, j)),
        out_shape=jax.ShapeDtypeStruct((depth, 8, n), F32),
        compiler_params=_params(("parallel", "parallel")),
        name="ada_modulation",
    )(cond8, ada_w, ada_b.reshape(depth, 1, n))


def _norm_kernel(*refs, gate_row, scale_row, shift_row, emit_x, transposed):
    it = iter(refs)
    x_ref = next(it)
    x = x_ref[...]
    if gate_row is not None:
        d_ref = next(it)
        pm_ref = next(it)
        x = x + pm_ref[gate_row:gate_row + 1, :] * d_ref[...]
    g_ref = next(it)
    m_ref = next(it) if scale_row is not None else None
    xo_ref = next(it) if emit_x else None
    h_ref = next(it)
    if emit_x:
        xo_ref[...] = x
    ms = jnp.mean(x * x, axis=-1, keepdims=True)
    y = x * lax.rsqrt(ms + EPS) * g_ref[...]
    if scale_row is not None:
        y = y * (1.0 + m_ref[scale_row:scale_row + 1, :]) + m_ref[shift_row:shift_row + 1, :]
    if transposed:
        y = y.T
    h_ref[...] = y.astype(h_ref.dtype)


def norm_modulate(x, g, *, delta=None, prev_mod=None, gate_row=None, mod=None, scale_row=None,
                  shift_row=None, emit_x=False, transposed=False, out_dtype=BF16, tr=256):
    t, d = x.shape
    row = pl.BlockSpec((tr, d), lambda i: (i, 0))
    modspec = pl.BlockSpec((None, MOD_ROWS, d), lambda i: ((i * tr) // SEG, 0, 0))
    args, specs = [x], [row]
    if delta is not None:
        args += [delta, prev_mod]
        specs += [row, modspec]
    args.append(g.reshape(1, d))
    specs.append(pl.BlockSpec((1, d), lambda i: (0, 0)))
    if mod is not None:
        args.append(mod)
        specs.append(modspec)
    if transposed:
        out_shape = [jax.ShapeDtypeStruct((d, t), out_dtype)]
        out_specs = [pl.BlockSpec((d, tr), lambda i: (0, i))]
    else:
        out_shape = [jax.ShapeDtypeStruct((t, d), out_dtype)]
        out_specs = [row]
    if emit_x:
        out_shape.insert(0, jax.ShapeDtypeStruct((t, d), F32))
        out_specs.insert(0, row)
    kern = functools.partial(_norm_kernel, gate_row=gate_row if delta is not None else None,
                             scale_row=scale_row if mod is not None else None,
                             shift_row=shift_row, emit_x=emit_x, transposed=transposed)
    out = pl.pallas_call(
        kern, grid=(t // tr,), in_specs=specs, out_specs=out_specs, out_shape=out_shape,
        compiler_params=_params(("parallel",)), name="norm_modulate",
    )(*args)
    return out if emit_x else out[0]


def _mm_kernel(*refs, gate_row):
    if gate_row is None:
        x_ref, w_ref, o_ref = refs
    else:
        x_ref, w_ref, r_ref, m_ref, o_ref = refs
    acc = jnp.dot(x_ref[...], w_ref[...], preferred_element_type=F32)
    if gate_row is not None:
        acc = r_ref[...] + m_ref[gate_row:gate_row + 1, :] * acc
    o_ref[...] = acc.astype(o_ref.dtype)


def matmul(x, w, *, out_dtype=F32, res=None, mod=None, gate_row=None, tm=1024, tn=512):
    m, k = x.shape
    n = w.shape[1]
    tm, tn = min(tm, m), min(tn, n)
    specs = [pl.BlockSpec((tm, k), lambda i, j: (i, 0)), pl.BlockSpec((k, tn), lambda i, j: (0, j))]
    args = [x, w]
    if res is not None:
        specs += [pl.BlockSpec((tm, tn), lambda i, j: (i, j)),
                  pl.BlockSpec((None, MOD_ROWS, tn), lambda i, j: ((i * tm) // SEG, 0, j))]
        args += [res, mod]
    return pl.pallas_call(
        functools.partial(_mm_kernel, gate_row=gate_row if res is not None else None),
        grid=(m // tm, n // tn), in_specs=specs,
        out_specs=pl.BlockSpec((tm, tn), lambda i, j: (i, j)),
        out_shape=jax.ShapeDtypeStruct((m, n), out_dtype),
        compiler_params=_params(("parallel", "arbitrary")), name="matmul",
    )(*args)


def _extract_top16(scores, index, rank_scale):
    work = scores
    rank = jnp.full(scores.shape, NO_RANK, F32)
    vals, idxs = [], []
    for i in range(PEER_TOPK):
        m = jnp.max(work, axis=0, keepdims=True)
        first = jnp.min(jnp.where(work == m, index, NO_RANK), axis=0, keepdims=True)
        sel = index == first
        rank = jnp.where(sel, rank_scale * i, rank)
        work = jnp.where(sel, NEG_INF, work)
        vals.append(m)
        idxs.append(first)
    return vals, idxs, rank


def _stack_rows(rows, lo, n):
    r_iota = lax.broadcasted_iota(jnp.int32, (n, LANES), 0)
    out = jnp.broadcast_to(rows[lo], (n, LANES))
    for r in range(1, n):
        out = jnp.where(r_iota == r, rows[lo + r], out)
    return out


def _route_kernel(qt_ref, k1_ref, k2_ref, s1_ref, r1_ref, e1_ref, s2_ref, c2_ref, e2_ref, tau_ref):
    key_iota = lax.broadcasted_iota(jnp.int32, (PEER_NK, LANES), 0).astype(F32)
    j16 = lax.broadcasted_iota(jnp.int32, (PEER_TOPK, LANES), 0).astype(F32)
    j8 = lax.broadcasted_iota(jnp.int32, (SUBLANES, LANES), 0).astype(F32)
    cidx_blocks = [j16]
    for i in range(1, SUBLANES):
        cidx_blocks.append(jnp.where(j8 < PEER_TOPK // (i + 1), PEER_TOPK * i + j8, NO_RANK))
    cidx_blocks.append((j8 + SUBLANES) * PEER_TOPK)
    cand_index = jnp.concatenate(cidx_blocks, axis=0)

    def head(h, carry):
        base = pl.multiple_of(h * PEER_DQ, PEER_DQ)
        q1 = qt_ref[pl.ds(base, PEER_HALF), :].astype(BF16)
        q2 = qt_ref[pl.ds(base + PEER_HALF, PEER_HALF), :].astype(BF16)
        s1 = jnp.dot(k1_ref[h].astype(BF16), q1, preferred_element_type=F32)
        s2 = jnp.dot(k2_ref[h].astype(BF16), q2, preferred_element_type=F32)
        v1, _, rank1 = _extract_top16(s1, key_iota, float(PEER_TOPK))
        v2, _, rank2 = _extract_top16(s2, key_iota, 1.0)
        v2_lo = _stack_rows(v2, 0, SUBLANES)
        cand_blocks = [v1[0] + _stack_rows(v2, 0, PEER_TOPK)]
        for i in range(1, SUBLANES):
            cand_blocks.append(jnp.where(j8 < PEER_TOPK // (i + 1), v1[i] + v2_lo, NEG_INF))
        cand_blocks.append(_stack_rows(v1, SUBLANES, SUBLANES) + v2[0])
        sc, ci, _ = _extract_top16(jnp.concatenate(cand_blocks, axis=0), cand_index, 1.0)
        z = jnp.ones_like(sc[0])
        for i in range(1, PEER_TOPK):
            z = z + jnp.exp(sc[i] - sc[0])
        s1_ref[h] = s1
        r1_ref[h] = rank1
        e1_ref[h] = jnp.exp(s1 - v1[0])
        s2_ref[h] = s2
        c2_ref[h] = ci[PEER_TOPK - 1] - rank2
        e2_ref[h] = jnp.exp(s2 - v2[0]) / z
        tau_ref[h] = sc[PEER_TOPK - 1]
        return carry

    lax.fori_loop(0, PEER_HEADS, head, 0)


def peer_route(qt, k1, k2):
    t = qt.shape[1]
    keys = pl.BlockSpec((PEER_HEADS, PEER_NK, PEER_HALF), lambda i: (0, 0, 0))
    table = pl.BlockSpec((PEER_HEADS, PEER_NK, LANES), lambda i: (0, 0, i))
    shape = jax.ShapeDtypeStruct((PEER_HEADS, PEER_NK, t), F32)
    return pl.pallas_call(
        _route_kernel, grid=(t // LANES,),
        in_specs=[pl.BlockSpec((PEER_HEADS * PEER_DQ, LANES), lambda i: (0, i)), keys, keys],
        out_specs=[table] * 6 + [pl.BlockSpec((PEER_HEADS, 1, LANES), lambda i: (0, 0, i))],
        out_shape=[shape] * 6 + [jax.ShapeDtypeStruct((PEER_HEADS, 1, t), F32)],
        compiler_params=_params(("parallel",)), name="peer_route",
    )(qt, k1, k2)


def _peer_kernel(ht_ref, u_ref, v_ref, s1_ref, r1_ref, e1_ref, s2_ref, c2_ref, e2_ref, tau_ref, o_ref, *, n_a):
    j = pl.program_id(1)

    @pl.when(j == 0)
    def _():
        o_ref[...] = jnp.zeros_like(o_ref)

    first_key = (j % (SUBLANES // n_a)) * n_a
    st = jnp.dot(u_ref[...], ht_ref[...], preferred_element_type=F32)
    act = 0.5 * st * (1.0 + lax.erf(st * INV_SQRT2))
    slabs = []
    for al in range(n_a):
        row = pl.ds(first_key + al, 1)
        g = None
        for h in range(PEER_HEADS):
            d = (s1_ref[h, row, :] + s2_ref[h]) - tau_ref[h]
            order = jnp.where(d == 0.0, c2_ref[h] - r1_ref[h, row, :], d)
            term = jnp.where(order >= 0.0, e1_ref[h, row, :] * e2_ref[h], 0.0)
            g = term if g is None else g + term
        slabs.append(g * act[al * PEER_NK:(al + 1) * PEER_NK, :])
    w = jnp.concatenate(slabs, axis=0).T.astype(BF16)
    o_ref[...] += jnp.dot(w, v_ref[...], preferred_element_type=F32)


def peer_dense(ht, u, v, s1, r1, e1, s2, c2, e2, tau, *, tm=512, n_a=4):
    d, t = ht.shape
    tn = n_a * PEER_NK
    nj = u.shape[0] // tn
    per_block = SUBLANES // n_a
    once = pl.Buffered(1)
    by_key = pl.BlockSpec((PEER_HEADS, SUBLANES, tm), lambda i, j: (0, j // per_block, i))
    by_tok = pl.BlockSpec((PEER_HEADS, PEER_NK, tm), lambda i, j: (0, 0, i), pipeline_mode=once)
    return pl.pallas_call(
        functools.partial(_peer_kernel, n_a=n_a),
        grid=(t // tm, nj),
        in_specs=[
            pl.BlockSpec((d, tm), lambda i, j: (0, i), pipeline_mode=once),
            pl.BlockSpec((tn, d), lambda i, j: (j, 0)),
            pl.BlockSpec((tn, d), lambda i, j: (j, 0)),
            by_key, by_key, by_key, by_tok, by_tok, by_tok,
            pl.BlockSpec((PEER_HEADS, 1, tm), lambda i, j: (0, 0, i), pipeline_mode=once),
        ],
        out_specs=pl.BlockSpec((tm, d), lambda i, j: (i, 0), pipeline_mode=once),
        out_shape=jax.ShapeDtypeStruct((t, d), F32),
        compiler_params=_params(("parallel", "arbitrary")), name="peer_dense",
    )(ht, u, v, s1, r1, e1, s2, c2, e2, tau)


def _to_chunks(t, c):
    b, l = t.shape[0], t.shape[1]
    return jnp.moveaxis(t.reshape((b, l // c, c) + t.shape[2:]), 1, 0)


def _hgrn_scan(q, k, v, logf, s0):
    bsz, seqlen = q.shape[0], q.shape[1]
    c = HGRN_CHUNK
    mask = jnp.tril(jnp.ones((c, c), dtype=bool))

    def step(s, inp):
        qc, kc, vc, ac = inp
        b = jnp.cumsum(ac, axis=1)
        bm = b[:, c // 2:c // 2 + 1]
        o_inter = jnp.einsum('bchk,bhkv->bchv', qc * jnp.exp(b), s)
        a = jnp.einsum('bthk,bshk->bhts', qc * jnp.exp(b - bm), kc * jnp.exp(bm - b))
        a = jnp.where(mask, a, 0.0)
        o_intra = jnp.einsum('bhts,bshv->bthv', a, vc)
        bl = b[:, -1]
        k_end = kc * jnp.exp(bl[:, None] - b)
        s_new = jnp.exp(bl)[..., None] * s + jnp.einsum('bshk,bshv->bhkv', k_end, vc)
        return s_new, o_inter + o_intra

    s_fin, o = lax.scan(step, s0.astype(F32),
                        (_to_chunks(q, c), _to_chunks(k, c), _to_chunks(v, c), _to_chunks(logf, c)))
    o = jnp.moveaxis(o, 0, 1).reshape(bsz, seqlen, q.shape[2], v.shape[3])
    return o, s_fin


def _even_mix(proj, layer, j, s0_f, s0_b, hgrn_lb, hgrn_onorm_g, gmlp_vnorm_g, gmlp_ws, gmlp_bs):
    bsz, seqlen, _ = proj.shape
    q, zf, zb, iv, g, u, v = jnp.split(
        proj, [A_W, 2 * A_W, 3 * A_W, 4 * A_W, 5 * A_W, 5 * A_W + B_W], axis=-1)
    hk = (bsz, seqlen, HGRN_H, HGRN_DK)
    qh = jax.nn.silu(q).reshape(hk)
    ih = iv.reshape(bsz, seqlen, HGRN_H, HGRN_DV)
    lb = jnp.cumsum(jax.nn.softmax(hgrn_lb.astype(F32), axis=0), axis=0)[layer]

    def gates(z, lbd):
        z = z.reshape(hk)
        return jnp.log(lbd + (1.0 - lbd) * jax.nn.sigmoid(z)), (1.0 - lbd) * jax.nn.sigmoid(-z)

    a_f, k_f = gates(zf, lb[0])
    a_b, k_b = gates(zb, lb[1])
    o_f, s_f = _hgrn_scan(qh, k_f, ih, a_f, s0_f)
    flip = lambda t: jnp.flip(t, axis=1)
    o_b, s_b = _hgrn_scan(flip(qh), flip(k_b), flip(ih), flip(a_b), s0_b)
    o = o_f + flip(o_b)
    o = o * lax.rsqrt(jnp.mean(o * o, axis=-1, keepdims=True) + EPS) * hgrn_onorm_g[j]
    out_a = o.reshape(bsz, seqlen, A_W) * jax.nn.silu(g)
    vv = v.reshape(bsz, seqlen // GMLP_CHUNK, GMLP_CHUNK, GMLP_H, GMLP_CH)
    vv = vv * lax.rsqrt(jnp.mean(vv * vv, axis=-1, keepdims=True) + EPS) * gmlp_vnorm_g[j]
    sp = jnp.einsum('hpq,bnqhc->bnphc', gmlp_ws[j], vv) + gmlp_bs[j].T[None, None, :, :, None]
    out_b = u * sp.reshape(bsz, seqlen, B_W)
    return jnp.concatenate([out_a, out_b], axis=-1).astype(BF16), s_f, s_b


def _odd_mix(proj, j, latent, conv_w, conv_b, conv_ln_g, conv_ln_b):
    bsz, seqlen, _ = proj.shape
    xc, a, gt = jnp.split(proj, [C_W, C_W + D_W], axis=-1)
    fc = jnp.fft.fft2(xc.reshape(bsz, seqlen, FNET_GROUPS, FNET_CH), axes=(1, 3), norm='ortho').real
    out_c = fc.reshape(bsz, seqlen, C_W)
    glu = a * jax.nn.sigmoid(gt)
    seqs = glu.reshape(bsz * (seqlen // GRID_W), GRID_W, D_W) if latent else glu
    y = lax.conv_general_dilated(seqs, conv_w[j][:, None, :], (1,), [(CONV_PAD, CONV_PAD)],
                                 dimension_numbers=('NWC', 'WIO', 'NWC'), feature_group_count=D_W)
    y = y.reshape(bsz, seqlen, D_W) + conv_b[j]
    yc = y - jnp.mean(y, axis=-1, keepdims=True)
    y = yc * lax.rsqrt(jnp.mean(yc * yc, axis=-1, keepdims=True) + EPS) * conv_ln_g[j] + conv_ln_b[j]
    return jnp.concatenate([out_c, jax.nn.silu(y)], axis=-1).astype(BF16)


def kernel(x_prompt, x_sample, state_hgrn, c, c_ctx, norm_g, ada_w, ada_b, even_w_in, hgrn_lb, hgrn_onorm_g, gmlp_vnorm_g, gmlp_ws, gmlp_bs, even_w_out, odd_w_in, conv_w, conv_b, conv_ln_g, conv_ln_b, odd_w_out, peer_wq, peer_k1, peer_k2, peer_u, peer_v, final_g):
    n_ctx = BATCH * SEQ
    x = jnp.concatenate([x_prompt.reshape(n_ctx, D_MODEL), x_sample.reshape(DEC_BATCH * DEC_SEQ, D_MODEL)], axis=0)

    cond8 = jnp.zeros((8, D_MODEL), F32).at[0].set(c_ctx).at[1:1 + DEC_BATCH].set(c)
    mod_all = ada_modulation(cond8, ada_w, ada_b)
    mod_all = mod_all[:, :NSEG].reshape(DEPTH, NSEG, 6, D_MODEL)
    mod_all = jnp.pad(mod_all, ((0, 0), (0, 0), (0, MOD_ROWS - 6), (0, 0)))

    states = []
    delta = prev_mod = None
    for l in range(DEPTH):
        mod = mod_all[l]
        j = l // 2
        if delta is None:
            h = norm_modulate(x, norm_g[l, 0], mod=mod, scale_row=1, shift_row=0)
        else:
            x, h = norm_modulate(x, norm_g[l, 0], delta=delta, prev_mod=prev_mod, gate_row=5,
                                 mod=mod, scale_row=1, shift_row=0, emit_x=True)
        if l % 2 == 0:
            proj = matmul(h, even_w_in[j].astype(BF16))
            ew = (hgrn_lb, hgrn_onorm_g, gmlp_vnorm_g, gmlp_ws, gmlp_bs)
            zero = jnp.zeros((BATCH, HGRN_H, HGRN_DK, HGRN_DV), F32)
            mix_c, s_f, s_b = _even_mix(proj[:n_ctx].reshape(BATCH, SEQ, -1), l, j, zero, zero, *ew)
            states.append(jnp.stack([s_f, s_b], axis=1))
            mix_l, _, _ = _even_mix(proj[n_ctx:].reshape(DEC_BATCH, DEC_SEQ, -1), l, j,
                                    state_hgrn[:, j, 0], state_hgrn[:, j, 1], *ew)
            w_out = even_w_out[j]
        else:
            proj = matmul(h, odd_w_in[j].astype(BF16))
            ow = (conv_w, conv_b, conv_ln_g, conv_ln_b)
            mix_c = _odd_mix(proj[:n_ctx].reshape(BATCH, SEQ, -1), j, False, *ow)
            mix_l = _odd_mix(proj[n_ctx:].reshape(DEC_BATCH, DEC_SEQ, -1), j, True, *ow)
            w_out = odd_w_out[j]
        mix = jnp.concatenate([mix_c.reshape(n_ctx, D_MODEL), mix_l.reshape(-1, D_MODEL)], axis=0)
        x = matmul(mix, w_out.astype(BF16), res=x, mod=mod, gate_row=2)

        ht = norm_modulate(x, norm_g[l, 1], mod=mod, scale_row=4, shift_row=3, transposed=True)
        qt = matmul(peer_wq[l].T.astype(BF16), ht)
        tables = peer_route(qt, peer_k1[l], peer_k2[l])
        delta = peer_dense(ht, peer_u[l].astype(BF16), peer_v[l].astype(BF16), *tables)
        prev_mod = mod

    y = norm_modulate(x, final_g, delta=delta, prev_mod=prev_mod, gate_row=5, out_dtype=F32)
    y_prompt = y[:n_ctx].reshape(BATCH, SEQ, D_MODEL)
    y_sample = y[n_ctx:].reshape(DEC_BATCH, DEC_SEQ, D_MODEL)
    return (y_prompt, y_sample, jnp.stack(states, axis=1))
```

```python
import functools

import jax
import jax.numpy as jnp
from jax import lax
from jax.experimental import pallas as pl
from jax.experimental.pallas import tpu as pltpu

F32 = jnp.float32
BF16 = jnp.bfloat16

D_MODEL = 4096
BATCH = 16
SEQ = 256
DEPTH = 2
DEC_BATCH = 4
DEC_SEQ = 4096
GRID_W = 64
EPS = 1e-6
A_W = D_MODEL // 2
B_W = D_MODEL // 2
C_W = D_MODEL // 2
D_W = D_MODEL // 2
HGRN_DK = 128
HGRN_H = A_W // HGRN_DK
HGRN_DV = A_W // HGRN_H
HGRN_CHUNK = 32
GMLP_CHUNK = 128
GMLP_CH = 128
GMLP_H = B_W // GMLP_CH
FNET_GROUPS = 4
FNET_CH = C_W // FNET_GROUPS
CONV_W = 31
CONV_PAD = (CONV_W - 1) // 2
PEER_HEADS = 8
PEER_DQ = 256
PEER_HALF = PEER_DQ // 2
PEER_NK = 128
PEER_TOPK = 16

SEG = 4096
NSEG = 1 + DEC_BATCH
TOKENS = NSEG * SEG
MOD_ROWS = 8
LANES = 128
SUBLANES = 8

VMEM_LIMIT = 56 * 1024 * 1024
INV_SQRT2 = 0.7071067811865476
NEG_INF = float("-inf")
NO_RANK = 1e9


def _params(sem):
    return pltpu.CompilerParams(dimension_semantics=sem, vmem_limit_bytes=VMEM_LIMIT)


def _ada_kernel(c_ref, w_ref, b_ref, o_ref):
    c = c_ref[...]
    a = (c * jax.nn.sigmoid(c)).astype(BF16)
    o_ref[...] = jnp.dot(a, w_ref[...].astype(BF16), preferred_element_type=F32) + b_ref[...]


def ada_modulation(cond8, ada_w, ada_b, tn=512):
    depth, d, n = ada_w.shape
    return pl.pallas_call(
        _ada_kernel,
        grid=(depth, n // tn),
        in_specs=[
            pl.BlockSpec((8, d), lambda l, j: (0, 0)),
            pl.BlockSpec((None, d, tn), lambda l, j: (l, 0, j)),
            pl.BlockSpec((None, 1, tn), lambda l, j: (l, 0, j)),
        ],
        out_specs=pl.BlockSpec((None, 8, tn), lambda l, j: (l, 0, j)),
        out_shape=jax.ShapeDtypeStruct((depth, 8, n), F32),
        compiler_params=_params(("parallel", "parallel")),
        name="ada_modulation",
    )(cond8, ada_w, ada_b.reshape(depth, 1, n))


def _norm_kernel(*refs, gate_row, scale_row, shift_row, emit_x, transposed):
    it = iter(refs)
    x_ref = next(it)
    x = x_ref[...]
    if gate_row is not None:
        d_ref = next(it)
        pm_ref = next(it)
        x = x + pm_ref[gate_row:gate_row + 1, :] * d_ref[...]
    g_ref = next(it)
    m_ref = next(it) if scale_row is not None else None
    xo_ref = next(it) if emit_x else None
    h_ref = next(it)
    if emit_x:
        xo_ref[...] = x
    ms = jnp.mean(x * x, axis=-1, keepdims=True)
    y = x * lax.rsqrt(ms + EPS) * g_ref[...]
    if scale_row is not None:
        y = y * (1.0 + m_ref[scale_row:scale_row + 1, :]) + m_ref[shift_row:shift_row + 1, :]
    if transposed:
        y = y.T
    h_ref[...] = y.astype(h_ref.dtype)


def norm_modulate(x, g, *, delta=None, prev_mod=None, gate_row=None, mod=None, scale_row=None,
                  shift_row=None, emit_x=False, transposed=False, out_dtype=BF16, tr=256):
    t, d = x.shape
    row = pl.BlockSpec((tr, d), lambda i: (i, 0))
    modspec = pl.BlockSpec((None, MOD_ROWS, d), lambda i: ((i * tr) // SEG, 0, 0))
    args, specs = [x], [row]
    if delta is not None:
        args += [delta, prev_mod]
        specs += [row, modspec]
    args.append(g.reshape(1, d))
    specs.append(pl.BlockSpec((1, d), lambda i: (0, 0)))
    if mod is not None:
        args.append(mod)
        specs.append(modspec)
    if transposed:
        out_shape = [jax.ShapeDtypeStruct((d, t), out_dtype)]
        out_specs = [pl.BlockSpec((d, tr), lambda i: (0, i))]
    else:
        out_shape = [jax.ShapeDtypeStruct((t, d), out_dtype)]
        out_specs = [row]
    if emit_x:
        out_shape.insert(0, jax.ShapeDtypeStruct((t, d), F32))
        out_specs.insert(0, row)
    kern = functools.partial(_norm_kernel, gate_row=gate_row if delta is not None else None,
                             scale_row=scale_row if mod is not None else None,
                             shift_row=shift_row, emit_x=emit_x, transposed=transposed)
    out = pl.pallas_call(
        kern, grid=(t // tr,), in_specs=specs, out_specs=out_specs, out_shape=out_shape,
        compiler_params=_params(("parallel",)), name="norm_modulate",
    )(*args)
    return out if emit_x else out[0]


def _mm_kernel(*refs, two, gate_row):
    it = iter(refs)
    x_ref = next(it)
    x2_ref = next(it) if two else None
    w_ref = next(it)
    w2_ref = next(it) if two else None
    acc = jnp.dot(x_ref[...], w_ref[...], preferred_element_type=F32)
    if two:
        acc = acc + jnp.dot(x2_ref[...], w2_ref[...], preferred_element_type=F32)
    if gate_row is not None:
        r_ref = next(it)
        m_ref = next(it)
        acc = r_ref[...] + m_ref[gate_row:gate_row + 1, :] * acc
    o_ref = next(it)
    o_ref[...] = acc.astype(o_ref.dtype)


def matmul(x, w, *, x2=None, out_dtype=F32, res=None, mod=None, gate_row=None, tm=1024, tn=512):
    m, k = x.shape
    n = w.shape[1]
    tm, tn = min(tm, m), min(tn, n)
    xspec = pl.BlockSpec((tm, k), lambda i, j: (i, 0))
    specs, args = [xspec], [x]
    if x2 is not None:
        specs += [xspec, pl.BlockSpec((k, tn), lambda i, j: (0, j)), pl.BlockSpec((k, tn), lambda i, j: (1, j))]
        args += [x2, w, w]
    else:
        specs.append(pl.BlockSpec((k, tn), lambda i, j: (0, j)))
        args.append(w)
    if res is not None:
        specs += [pl.BlockSpec((tm, tn), lambda i, j: (i, j)),
                  pl.BlockSpec((None, MOD_ROWS, tn), lambda i, j: ((i * tm) // SEG, 0, j))]
        args += [res, mod]
    return pl.pallas_call(
        functools.partial(_mm_kernel, two=x2 is not None, gate_row=gate_row if res is not None else None),
        grid=(m // tm, n // tn), in_specs=specs,
        out_specs=pl.BlockSpec((tm, tn), lambda i, j: (i, j)),
        out_shape=jax.ShapeDtypeStruct((m, n), out_dtype),
        compiler_params=_params(("parallel", "arbitrary")), name="matmul",
    )(*args)


def _extract_top16(scores, index, rank_scale):
    work = scores
    rank = jnp.full(scores.shape, NO_RANK, F32)
    vals, idxs = [], []
    for i in range(PEER_TOPK):
        m = jnp.max(work, axis=0, keepdims=True)
        first = jnp.min(jnp.where(work == m, index, NO_RANK), axis=0, keepdims=True)
        sel = index == first
        rank = jnp.where(sel, rank_scale * i, rank)
        work = jnp.where(sel, NEG_INF, work)
        vals.append(m)
        idxs.append(first)
    return vals, idxs, rank


def _stack_rows(rows, lo, n):
    r_iota = lax.broadcasted_iota(jnp.int32, (n, LANES), 0)
    out = jnp.broadcast_to(rows[lo], (n, LANES))
    for r in range(1, n):
        out = jnp.where(r_iota == r, rows[lo + r], out)
    return out


def _route_kernel(qt_ref, k1_ref, k2_ref, s1_ref, r1_ref, e1_ref, s2_ref, c2_ref, e2_ref, tau_ref):
    key_iota = lax.broadcasted_iota(jnp.int32, (PEER_NK, LANES), 0).astype(F32)
    j16 = lax.broadcasted_iota(jnp.int32, (PEER_TOPK, LANES), 0).astype(F32)
    j8 = lax.broadcasted_iota(jnp.int32, (SUBLANES, LANES), 0).astype(F32)
    cidx_blocks = [j16]
    for i in range(1, SUBLANES):
        cidx_blocks.append(jnp.where(j8 < PEER_TOPK // (i + 1), PEER_TOPK * i + j8, NO_RANK))
    cidx_blocks.append((j8 + SUBLANES) * PEER_TOPK)
    cand_index = jnp.concatenate(cidx_blocks, axis=0)

    def head(h, carry):
        base = pl.multiple_of(h * PEER_DQ, PEER_DQ)
        q1 = qt_ref[pl.ds(base, PEER_HALF), :].astype(BF16)
        q2 = qt_ref[pl.ds(base + PEER_HALF, PEER_HALF), :].astype(BF16)
        s1 = jnp.dot(k1_ref[h].astype(BF16), q1, preferred_element_type=F32)
        s2 = jnp.dot(k2_ref[h].astype(BF16), q2, preferred_element_type=F32)
        v1, _, rank1 = _extract_top16(s1, key_iota, float(PEER_TOPK))
        v2, _, rank2 = _extract_top16(s2, key_iota, 1.0)
        v2_lo = _stack_rows(v2, 0, SUBLANES)
        cand_blocks = [v1[0] + _stack_rows(v2, 0, PEER_TOPK)]
        for i in range(1, SUBLANES):
            cand_blocks.append(jnp.where(j8 < PEER_TOPK // (i + 1), v1[i] + v2_lo, NEG_INF))
        cand_blocks.append(_stack_rows(v1, SUBLANES, SUBLANES) + v2[0])
        sc, ci, _ = _extract_top16(jnp.concatenate(cand_blocks, axis=0), cand_index, 1.0)
        z = jnp.ones_like(sc[0])
        for i in range(1, PEER_TOPK):
            z = z + jnp.exp(sc[i] - sc[0])
        s1_ref[h] = s1
        r1_ref[h] = rank1
        e1_ref[h] = jnp.exp(s1 - v1[0])
        s2_ref[h] = s2
        c2_ref[h] = ci[PEER_TOPK - 1] - rank2
        e2_ref[h] = jnp.exp(s2 - v2[0]) / z
        tau_ref[h] = sc[PEER_TOPK - 1]
        return carry

    lax.fori_loop(0, PEER_HEADS, head, 0)


def peer_route(qt, k1, k2):
    t = qt.shape[1]
    keys = pl.BlockSpec((PEER_HEADS, PEER_NK, PEER_HALF), lambda i: (0, 0, 0))
    table = pl.BlockSpec((PEER_HEADS, PEER_NK, LANES), lambda i: (0, 0, i))
    shape = jax.ShapeDtypeStruct((PEER_HEADS, PEER_NK, t), F32)
    return pl.pallas_call(
        _route_kernel, grid=(t // LANES,),
        in_specs=[pl.BlockSpec((PEER_HEADS * PEER_DQ, LANES), lambda i: (0, i)), keys, keys],
        out_specs=[table] * 6 + [pl.BlockSpec((PEER_HEADS, 1, LANES), lambda i: (0, 0, i))],
        out_shape=[shape] * 6 + [jax.ShapeDtypeStruct((PEER_HEADS, 1, t), F32)],
        compiler_params=_params(("parallel",)), name="peer_route",
    )(qt, k1, k2)


def _peer_kernel(ht_ref, u_ref, v_ref, s1_ref, r1_ref, e1_ref, s2_ref, c2_ref, e2_ref, tau_ref, o_ref, w_ref, *, n_a):
    j = pl.program_id(1)

    @pl.when(j == 0)
    def _():
        o_ref[...] = jnp.zeros_like(o_ref)
        w_ref[1] = jnp.zeros(w_ref.shape[1:], w_ref.dtype)

    st = jnp.dot(u_ref[...], ht_ref[...], preferred_element_type=F32)
    o_ref[...] += jnp.dot(w_ref[(j + 1) % 2], v_ref[...], preferred_element_type=F32)

    tile = jnp.minimum(j, pl.num_programs(1) - 2)
    first_key = (tile % (SUBLANES // n_a)) * n_a
    act = 0.5 * st * (1.0 + lax.erf(st * INV_SQRT2))
    slabs = []
    for al in range(n_a):
        row = pl.ds(first_key + al, 1)
        g = None
        for h in range(PEER_HEADS):
            d = (s1_ref[h, row, :] + s2_ref[h]) - tau_ref[h]
            order = jnp.where(d == 0.0, c2_ref[h] - r1_ref[h, row, :], d)
            term = jnp.where(order >= 0.0, e1_ref[h, row, :] * e2_ref[h], 0.0)
            g = term if g is None else g + term
        slabs.append(g * act[al * PEER_NK:(al + 1) * PEER_NK, :])
    w_ref[j % 2] = jnp.concatenate(slabs, axis=0).T.astype(BF16)


def peer_dense(ht, u, v, s1, r1, e1, s2, c2, e2, tau, *, tm=512, n_a=4):
    d, t = ht.shape
    tn = n_a * PEER_NK
    nj = u.shape[0] // tn
    per_block = SUBLANES // n_a
    once = pl.Buffered(1)
    build = lambda j: jnp.minimum(j, nj - 1)
    by_key = pl.BlockSpec((PEER_HEADS, SUBLANES, tm), lambda i, j: (0, build(j) // per_block, i))
    by_tok = pl.BlockSpec((PEER_HEADS, PEER_NK, tm), lambda i, j: (0, 0, i), pipeline_mode=once)
    return pl.pallas_call(
        functools.partial(_peer_kernel, n_a=n_a),
        grid=(t // tm, nj + 1),
        in_specs=[
            pl.BlockSpec((d, tm), lambda i, j: (0, i), pipeline_mode=once),
            pl.BlockSpec((tn, d), lambda i, j: (build(j), 0)),
            pl.BlockSpec((tn, d), lambda i, j: (jnp.maximum(j - 1, 0), 0)),
            by_key, by_key, by_key, by_tok, by_tok, by_tok,
            pl.BlockSpec((PEER_HEADS, 1, tm), lambda i, j: (0, 0, i), pipeline_mode=once),
        ],
        out_specs=pl.BlockSpec((tm, d), lambda i, j: (i, 0), pipeline_mode=once),
        out_shape=jax.ShapeDtypeStruct((t, d), F32),
        scratch_shapes=[pltpu.VMEM((2, tm, tn), BF16)],
        compiler_params=_params(("parallel", "arbitrary")), name="peer_dense",
    )(ht, u, v, s1, r1, e1, s2, c2, e2, tau)


HGRN_BLOCK = 256
HGRN_GROUP = 4
N_SEQ = BATCH + DEC_BATCH


def _split3(x):
    hi = x.astype(BF16)
    r = x - hi.astype(F32)
    mid = r.astype(BF16)
    return hi, mid, (r - mid.astype(F32)).astype(BF16)


def _hgrn_kernel(*refs, reverse, final):
    if final:
        q_ref, z_ref, v_ref, lb_ref, tri_ref, s0_ref, of_ref, g_ref, gn_ref, o_ref, sf_ref, st_ref = refs
    else:
        q_ref, z_ref, v_ref, lb_ref, tri_ref, s0_ref, o_ref, sf_ref, st_ref = refs
    r, p = pl.program_id(0), pl.program_id(1)
    rr = pl.num_programs(0) - 1 - r if reverse else r
    per_seq = DEC_SEQ // HGRN_BLOCK
    pos = (rr - BATCH) % per_seq
    starts = (rr < BATCH) | (pos == (per_seq - 1 if reverse else 0))

    @pl.when(starts)
    def _():
        for hh in range(HGRN_GROUP):
            st_ref[p * HGRN_GROUP + hh] = s0_ref[hh]

    q = q_ref[...]
    qs = q * jax.nn.sigmoid(q)
    z = z_ref[...]
    lb = lb_ref[...]
    logf = jnp.log(lb + (1.0 - lb) * jax.nn.sigmoid(z))
    k = (1.0 - lb) * jax.nn.sigmoid(-z)
    v = v_ref[...].astype(BF16)
    tri = tri_ref[...]
    b = sum(jnp.dot(tri, part, preferred_element_type=F32) for part in _split3(logf))

    c = HGRN_CHUNK
    t_io = lax.broadcasted_iota(jnp.int32, (c, c), 0)
    s_io = lax.broadcasted_iota(jnp.int32, (c, c), 1)
    keep = (s_io >= t_io) if reverse else (s_io <= t_io)
    mid = c // 2 - 1 if reverse else c // 2
    last = 0 if reverse else c - 1
    n_chunks = HGRN_BLOCK // c
    order = range(n_chunks - 1, -1, -1) if reverse else range(n_chunks)
    nt = (((1,), (1,)), ((), ()))
    tn = (((0,), (0,)), ((), ()))
    for hh in range(HGRN_GROUP):
        lanes = slice(hh * HGRN_DK, (hh + 1) * HGRN_DK)
        st = st_ref[p * HGRN_GROUP + hh]
        for ci in order:
            rows = slice(ci * c, (ci + 1) * c)
            bc, qc, kc, vc = b[rows, lanes], qs[rows, lanes], k[rows, lanes], v[rows, lanes]
            bm, bl = bc[mid:mid + 1], bc[last:last + 1]
            o = lax.dot_general((qc * jnp.exp(bc)).astype(BF16), st.astype(BF16), nt, preferred_element_type=F32)
            a = lax.dot_general((qc * jnp.exp(bc - bm)).astype(BF16), (kc * jnp.exp(bm - bc)).astype(BF16), nt,
                                preferred_element_type=F32)
            a = jnp.where(keep, a, 0.0).astype(BF16)
            o = o + jnp.dot(a, vc, preferred_element_type=F32)
            k_end = (kc * jnp.exp(bl - bc)).astype(BF16)
            st = st * jnp.exp(bl) + lax.dot_general(vc, k_end, tn, preferred_element_type=F32)
            if final:
                o = o + of_ref[rows, lanes]
                o = o * lax.rsqrt(jnp.mean(o * o, axis=-1, keepdims=True) + EPS) * gn_ref[...]
                gc = g_ref[rows, lanes]
                o = o * (gc * jax.nn.sigmoid(gc))
            o_ref[rows, lanes] = o.astype(o_ref.dtype)
        st_ref[p * HGRN_GROUP + hh] = st
        sf_ref[hh] = st


def _chunk_tri(reverse):
    i = jnp.arange(HGRN_BLOCK)
    same = (i[:, None] // HGRN_CHUNK) == (i[None, :] // HGRN_CHUNK)
    tri = (i[None, :] >= i[:, None]) if reverse else (i[None, :] <= i[:, None])
    return (same & tri).astype(BF16)


def _hgrn_pass(proj, z_col, lb, s0, *, reverse, o_prev=None, g_col=None, onorm=None):
    t = proj.shape[0]
    w = HGRN_GROUP * HGRN_DK
    per_group = A_W // w
    n_blk = t // HGRN_BLOCK
    blk = (lambda r: n_blk - 1 - r) if reverse else (lambda r: r)
    seq = lambda r: jnp.where(blk(r) < BATCH, blk(r), BATCH + (blk(r) - BATCH) // (DEC_SEQ // HGRN_BLOCK))
    col = lambda g: pl.BlockSpec((HGRN_BLOCK, w), lambda r, p: (blk(r), g * per_group + p))
    state = pl.BlockSpec((None, HGRN_GROUP, HGRN_DV, HGRN_DK), lambda r, p: (seq(r), p, 0, 0))
    final = o_prev is not None
    specs = [col(0), col(z_col), col(3), pl.BlockSpec((1, w), lambda r, p: (0, p)),
             pl.BlockSpec((HGRN_BLOCK, HGRN_BLOCK), lambda r, p: (0, 0)), state]
    args = [proj, proj, proj, lb, _chunk_tri(reverse), s0]
    if final:
        specs += [pl.BlockSpec((HGRN_BLOCK, w), lambda r, p: (blk(r), p)), col(g_col),
                  pl.BlockSpec((1, HGRN_DV), lambda r, p: (0, 0))]
        args += [o_prev, proj, onorm]
    return pl.pallas_call(
        functools.partial(_hgrn_kernel, reverse=reverse, final=final),
        grid=(n_blk, per_group), in_specs=specs,
        out_specs=[pl.BlockSpec((HGRN_BLOCK, w), lambda r, p: (blk(r), p)),
                   pl.BlockSpec((None, HGRN_GROUP, HGRN_DV, HGRN_DK), lambda r, p: (blk(r), p, 0, 0))],
        out_shape=[jax.ShapeDtypeStruct((t, A_W), BF16 if final else F32),
                   jax.ShapeDtypeStruct((n_blk, HGRN_H, HGRN_DV, HGRN_DK), F32)],
        scratch_shapes=[pltpu.VMEM((HGRN_H, HGRN_DV, HGRN_DK), F32)],
        compiler_params=_params(("arbitrary", "arbitrary")), name="hgrn_scan",
    )(*args)


def _gmlp_kernel(u_ref, v_ref, ws_ref, bs_ref, gn_ref, o_ref):
    for h in range(GMLP_H):
        lanes = slice(h * GMLP_CH, (h + 1) * GMLP_CH)
        vv = v_ref[:, lanes]
        vn = vv * lax.rsqrt(jnp.mean(vv * vv, axis=-1, keepdims=True) + EPS) * gn_ref[h:h + 1, :]
        sp = jnp.dot(ws_ref[h], vn.astype(BF16), preferred_element_type=F32) + bs_ref[h]
        o_ref[:, lanes] = (u_ref[:, lanes] * sp).astype(o_ref.dtype)


def gmlp_mix(proj, ws, bs, vnorm_g):
    t = proj.shape[0]
    u_col, v_col = (5 * A_W) // B_W, (5 * A_W + B_W) // B_W
    full = lambda shape: pl.BlockSpec(shape, lambda i: (0,) * len(shape))
    return pl.pallas_call(
        _gmlp_kernel, grid=(t // GMLP_CHUNK,),
        in_specs=[pl.BlockSpec((GMLP_CHUNK, B_W), lambda i: (i, u_col)),
                  pl.BlockSpec((GMLP_CHUNK, B_W), lambda i: (i, v_col)),
                  full((GMLP_H, GMLP_CHUNK, GMLP_CHUNK)), full((GMLP_H, GMLP_CHUNK, GMLP_CH)),
                  full((GMLP_H, GMLP_CH))],
        out_specs=pl.BlockSpec((GMLP_CHUNK, B_W), lambda i: (i, 0)),
        out_shape=jax.ShapeDtypeStruct((t, B_W), BF16),
        compiler_params=_params(("parallel",)), name="gmlp_mix",
    )(proj, proj, ws.astype(BF16), jnp.broadcast_to(bs[:, :, None], (GMLP_H, GMLP_CHUNK, GMLP_CH)), vnorm_g)


def _dft_matrices(n, scale):
    j = jnp.arange(n, dtype=jnp.int32)
    ang = ((j[:, None] * j[None, :]) % n).astype(F32) * (2.0 * jnp.pi / n)
    return (jnp.cos(ang) * scale).astype(BF16), (jnp.sin(ang) * scale).astype(BF16)


def _dft_ch_kernel(x_ref, w_ref, o_ref):
    o_ref[...] = jnp.dot(x_ref[...].astype(BF16), w_ref[...], preferred_element_type=F32).astype(o_ref.dtype)


def dft_channels(proj, tm=1024):
    t = proj.shape[0]
    cs, sn = _dft_matrices(FNET_CH, FNET_CH ** -0.5)
    return pl.pallas_call(
        _dft_ch_kernel, grid=(t // tm, FNET_GROUPS),
        in_specs=[pl.BlockSpec((tm, FNET_CH), lambda i, g: (i, g)),
                  pl.BlockSpec((FNET_CH, 2 * FNET_CH), lambda i, g: (0, 0))],
        out_specs=pl.BlockSpec((tm, 2 * FNET_CH), lambda i, g: (i, g)),
        out_shape=jax.ShapeDtypeStruct((t, 2 * C_W), BF16),
        compiler_params=_params(("parallel", "parallel")), name="dft_channels",
    )(proj, jnp.concatenate([cs, sn], axis=1))


def _dft_seq_kernel(*refs):
    c_ref, s_ref, p_ref, q_ref = refs[:4]
    o_ref = refs[-1]
    acc = jnp.dot(c_ref[...], p_ref[...], preferred_element_type=F32)
    acc = acc - jnp.dot(s_ref[...], q_ref[...], preferred_element_type=F32)
    o_ref[...] = acc.astype(o_ref.dtype)


def dft_sequence(pq, seq_len, n_seq, first_row, out=None, tm=1024, tn=256):
    t = pq.shape[0]
    tm = min(tm, seq_len)
    per_group = FNET_CH // tn
    cs, sn = _dft_matrices(seq_len, seq_len ** -0.5)
    base = first_row // seq_len
    pcol = lambda j: (j // per_group) * 2 * per_group + j % per_group
    specs = [pl.BlockSpec((tm, seq_len), lambda b, m, j: (m, 0)),
             pl.BlockSpec((tm, seq_len), lambda b, m, j: (m, 0)),
             pl.BlockSpec((seq_len, tn), lambda b, m, j: (base + b, pcol(j))),
             pl.BlockSpec((seq_len, tn), lambda b, m, j: (base + b, pcol(j) + per_group))]
    args = [cs, sn, pq, pq]
    aliases = {}
    if out is not None:
        specs.append(pl.BlockSpec(memory_space=pl.ANY))
        args.append(out)
        aliases = {4: 0}
    rows_per_seq = seq_len // tm
    return pl.pallas_call(
        _dft_seq_kernel, grid=(n_seq, rows_per_seq, C_W // tn), in_specs=specs,
        out_specs=pl.BlockSpec((tm, tn), lambda b, m, j: ((base + b) * rows_per_seq + m, j)),
        out_shape=jax.ShapeDtypeStruct((t, C_W), BF16), input_output_aliases=aliases,
        compiler_params=_params(("parallel", "parallel", "arbitrary")), name="dft_sequence",
    )(*args)


CONV_BLOCK = 256
CONV_HALO = 16


def _conv_kernel(a_ref, gt_ref, w_ref, cb_ref, lg_ref, lb_ref, o_ref, pad_c, pad_l, y_ref):
    r = pl.program_id(0)
    n_lanes = D_W // LANES

    def depthwise(pad, seg, n_rows, out_row):
        def chunk(ci, carry):
            lanes = pl.ds(pl.multiple_of(ci * LANES, LANES), LANES)
            acc = jnp.zeros((n_rows, LANES), F32)
            for tap in range(CONV_W):
                acc = acc + w_ref[tap:tap + 1, lanes] * pad[seg, pl.ds(CONV_HALO - CONV_PAD + tap, n_rows), lanes]
            y_ref[pl.ds(out_row, n_rows), lanes] = acc + cb_ref[:, lanes]
            return carry
        lax.fori_loop(0, n_lanes, chunk, 0)

    glu = a_ref[...] * jax.nn.sigmoid(gt_ref[...])
    zeros = jnp.zeros((CONV_HALO, D_W), F32)

    @pl.when(r < BATCH)
    def _():
        pad_c[0, 0:CONV_HALO, :] = zeros
        pad_c[0, CONV_HALO:CONV_HALO + CONV_BLOCK, :] = glu
        pad_c[0, CONV_HALO + CONV_BLOCK:, :] = zeros
        depthwise(pad_c, 0, CONV_BLOCK, 0)

    @pl.when(r >= BATCH)
    def _():
        for s in range(CONV_BLOCK // GRID_W):
            pad_l[s, 0:CONV_HALO, :] = zeros
            pad_l[s, CONV_HALO:CONV_HALO + GRID_W, :] = glu[s * GRID_W:(s + 1) * GRID_W]
            pad_l[s, CONV_HALO + GRID_W:, :] = zeros
            depthwise(pad_l, s, GRID_W, s * GRID_W)

    y = y_ref[...]
    yc = y - jnp.mean(y, axis=-1, keepdims=True)
    yn = yc * lax.rsqrt(jnp.mean(yc * yc, axis=-1, keepdims=True) + EPS) * lg_ref[...] + lb_ref[...]
    o_ref[...] = (yn * jax.nn.sigmoid(yn)).astype(o_ref.dtype)


def conv_module(proj, conv_w, conv_b, ln_g, ln_b):
    t = proj.shape[0]
    a_col, g_col = C_W // D_W, (C_W + D_W) // D_W
    row = pl.BlockSpec((1, D_W), lambda r: (0, 0))
    w_pad = jnp.pad(conv_w, ((0, 4 * SUBLANES - CONV_W), (0, 0)))
    return pl.pallas_call(
        _conv_kernel, grid=(t // CONV_BLOCK,),
        in_specs=[pl.BlockSpec((CONV_BLOCK, D_W), lambda r: (r, a_col)),
                  pl.BlockSpec((CONV_BLOCK, D_W), lambda r: (r, g_col)),
                  pl.BlockSpec((4 * SUBLANES, D_W), lambda r: (0, 0)), row, row, row],
        out_specs=pl.BlockSpec((CONV_BLOCK, D_W), lambda r: (r, 0)),
        out_shape=jax.ShapeDtypeStruct((t, D_W), BF16),
        scratch_shapes=[pltpu.VMEM((1, CONV_BLOCK + 2 * CONV_HALO, D_W), F32),
                        pltpu.VMEM((CONV_BLOCK // GRID_W, GRID_W + 2 * CONV_HALO, D_W), F32),
                        pltpu.VMEM((CONV_BLOCK, D_W), F32)],
        compiler_params=_params(("parallel",)), name="conv_module",
    )(proj, proj, w_pad, conv_b.reshape(1, D_W), ln_g.reshape(1, D_W), ln_b.reshape(1, D_W))


def kernel(x_prompt, x_sample, state_hgrn, c, c_ctx, norm_g, ada_w, ada_b, even_w_in, hgrn_lb, hgrn_onorm_g, gmlp_vnorm_g, gmlp_ws, gmlp_bs, even_w_out, odd_w_in, conv_w, conv_b, conv_ln_g, conv_ln_b, odd_w_out, peer_wq, peer_k1, peer_k2, peer_u, peer_v, final_g):
    n_ctx = BATCH * SEQ
    x = jnp.concatenate([x_prompt.reshape(n_ctx, D_MODEL), x_sample.reshape(DEC_BATCH * DEC_SEQ, D_MODEL)], axis=0)

    cond8 = jnp.zeros((8, D_MODEL), F32).at[0].set(c_ctx).at[1:1 + DEC_BATCH].set(c)
    mod_all = ada_modulation(cond8, ada_w, ada_b)
    mod_all = mod_all[:, :NSEG].reshape(DEPTH, NSEG, 6, D_MODEL)
    mod_all = jnp.pad(mod_all, ((0, 0), (0, 0), (0, MOD_ROWS - 6), (0, 0)))
    floor = jnp.cumsum(jax.nn.softmax(hgrn_lb.astype(F32), axis=0), axis=0)

    states = []
    delta = prev_mod = None
    for l in range(DEPTH):
        mod = mod_all[l]
        j = l // 2
        if delta is None:
            h = norm_modulate(x, norm_g[l, 0], mod=mod, scale_row=1, shift_row=0)
        else:
            x, h = norm_modulate(x, norm_g[l, 0], delta=delta, prev_mod=prev_mod, gate_row=5,
                                 mod=mod, scale_row=1, shift_row=0, emit_x=True)
        if l % 2 == 0:
            proj = matmul(h, even_w_in[j].astype(BF16))
            zero = jnp.zeros((BATCH, HGRN_H, HGRN_DV, HGRN_DK), F32)
            start = lambda d: jnp.concatenate([zero, jnp.swapaxes(state_hgrn[:, j, d], -1, -2)], axis=0)
            o_f, s_f = _hgrn_pass(proj, 1, floor[l, 0].reshape(1, A_W), start(0), reverse=False)
            mix_a, s_b = _hgrn_pass(proj, 2, floor[l, 1].reshape(1, A_W), start(1), reverse=True,
                                    o_prev=o_f, g_col=4, onorm=hgrn_onorm_g[j].reshape(1, HGRN_DV))
            states.append(jnp.swapaxes(jnp.stack([s_f[:BATCH], s_b[:BATCH]], axis=1), -1, -2))
            mix_b = gmlp_mix(proj, gmlp_ws[j], gmlp_bs[j], gmlp_vnorm_g[j])
            w_out = even_w_out[j]
        else:
            proj = matmul(h, odd_w_in[j].astype(BF16))
            pq = dft_channels(proj)
            mix_a = dft_sequence(pq, SEQ, BATCH, 0)
            mix_a = dft_sequence(pq, DEC_SEQ, DEC_BATCH, n_ctx, out=mix_a)
            mix_b = conv_module(proj, conv_w[j], conv_b[j], conv_ln_g[j], conv_ln_b[j])
            w_out = odd_w_out[j]
        x = matmul(mix_a, w_out.astype(BF16), x2=mix_b, res=x, mod=mod, gate_row=2)

        ht = norm_modulate(x, norm_g[l, 1], mod=mod, scale_row=4, shift_row=3, transposed=True)
        qt = matmul(peer_wq[l].T.astype(BF16), ht)
        tables = peer_route(qt, peer_k1[l], peer_k2[l])
        delta = peer_dense(ht, peer_u[l].astype(BF16), peer_v[l].astype(BF16), *tables)
        prev_mod = mod

    y = norm_modulate(x, final_g, delta=delta, prev_mod=prev_mod, gate_row=5, out_dtype=F32)
    y_prompt = y[:n_ctx].reshape(BATCH, SEQ, D_MODEL)
    y_sample = y[n_ctx:].reshape(DEC_BATCH, DEC_SEQ, D_MODEL)
    return (y_prompt, y_sample, jnp.stack(states, axis=1))
```

```python
import functools

import jax
import jax.numpy as jnp
from jax import lax
from jax.experimental import pallas as pl
from jax.experimental.pallas import tpu as pltpu

F32 = jnp.float32
BF16 = jnp.bfloat16

D_MODEL = 4096
BATCH = 16
SEQ = 256
DEPTH = 2
DEC_BATCH = 4
DEC_SEQ = 4096
GRID_W = 64
EPS = 1e-6
A_W = D_MODEL // 2
B_W = D_MODEL // 2
C_W = D_MODEL // 2
D_W = D_MODEL // 2
HGRN_DK = 128
HGRN_H = A_W // HGRN_DK
HGRN_DV = A_W // HGRN_H
HGRN_CHUNK = 32
GMLP_CHUNK = 128
GMLP_CH = 128
GMLP_H = B_W // GMLP_CH
FNET_GROUPS = 4
FNET_CH = C_W // FNET_GROUPS
CONV_W = 31
CONV_PAD = (CONV_W - 1) // 2
PEER_HEADS = 8
PEER_DQ = 256
PEER_HALF = PEER_DQ // 2
PEER_NK = 128
PEER_TOPK = 16

SEG = 4096
NSEG = 1 + DEC_BATCH
TOKENS = NSEG * SEG
MOD_ROWS = 8
LANES = 128
SUBLANES = 8

VMEM_LIMIT = 56 * 1024 * 1024
INV_SQRT2 = 0.7071067811865476
NEG_INF = float("-inf")
NO_RANK = 1e9


def _params(sem):
    return pltpu.CompilerParams(dimension_semantics=sem, vmem_limit_bytes=VMEM_LIMIT)


def _ada_kernel(c_ref, w_ref, b_ref, o_ref):
    c = c_ref[...]
    a = (c * jax.nn.sigmoid(c)).astype(BF16)
    o_ref[...] = jnp.dot(a, w_ref[...].astype(BF16), preferred_element_type=F32) + b_ref[...]


def ada_modulation(cond8, ada_w, ada_b, tn=512):
    depth, d, n = ada_w.shape
    return pl.pallas_call(
        _ada_kernel,
        grid=(depth, n // tn),
        in_specs=[
            pl.BlockSpec((8, d), lambda l, j: (0, 0)),
            pl.BlockSpec((None, d, tn), lambda l, j: (l, 0, j)),
            pl.BlockSpec((None, 1, tn), lambda l, j: (l, 0, j)),
        ],
        out_specs=pl.BlockSpec((None, 8, tn), lambda l, j: (l, 0, j)),
        out_shape=jax.ShapeDtypeStruct((depth, 8, n), F32),
        compiler_params=_params(("parallel", "parallel")),
        name="ada_modulation",
    )(cond8, ada_w, ada_b.reshape(depth, 1, n))


def _norm_kernel(*refs, gate_row, scale_row, shift_row, emit_x, transposed):
    it = iter(refs)
    x_ref = next(it)
    x = x_ref[...]
    if gate_row is not None:
        d_ref = next(it)
        pm_ref = next(it)
        x = x + pm_ref[gate_row:gate_row + 1, :] * d_ref[...]
    g_ref = next(it)
    m_ref = next(it) if scale_row is not None else None
    xo_ref = next(it) if emit_x else None
    h_ref = next(it)
    if emit_x:
        xo_ref[...] = x
    ms = jnp.mean(x * x, axis=-1, keepdims=True)
    y = x * lax.rsqrt(ms + EPS) * g_ref[...]
    if scale_row is not None:
        y = y * (1.0 + m_ref[scale_row:scale_row + 1, :]) + m_ref[shift_row:shift_row + 1, :]
    if transposed:
        y = y.T
    h_ref[...] = y.astype(h_ref.dtype)


def norm_modulate(x, g, *, delta=None, prev_mod=None, gate_row=None, mod=None, scale_row=None,
                  shift_row=None, emit_x=False, transposed=False, out_dtype=BF16, rows=None, tr=256):
    d = x.shape[1]
    first, t = rows if rows is not None else (0, x.shape[0])
    blk0 = first // tr
    row = pl.BlockSpec((tr, d), lambda i: (i, 0))
    in_row = pl.BlockSpec((tr, d), lambda i: (blk0 + i, 0))
    modspec = pl.BlockSpec((None, MOD_ROWS, d), lambda i: (((blk0 + i) * tr) // SEG, 0, 0))
    args, specs = [x], [in_row]
    if delta is not None:
        args += [delta, prev_mod]
        specs += [in_row, modspec]
    args.append(g.reshape(1, d))
    specs.append(pl.BlockSpec((1, d), lambda i: (0, 0)))
    if mod is not None:
        args.append(mod)
        specs.append(modspec)
    if transposed:
        out_shape = [jax.ShapeDtypeStruct((d, t), out_dtype)]
        out_specs = [pl.BlockSpec((d, tr), lambda i: (0, i))]
    else:
        out_shape = [jax.ShapeDtypeStruct((t, d), out_dtype)]
        out_specs = [row]
    if emit_x:
        out_shape.insert(0, jax.ShapeDtypeStruct((t, d), F32))
        out_specs.insert(0, row)
    kern = functools.partial(_norm_kernel, gate_row=gate_row if delta is not None else None,
                             scale_row=scale_row if mod is not None else None,
                             shift_row=shift_row, emit_x=emit_x, transposed=transposed)
    out = pl.pallas_call(
        kern, grid=(t // tr,), in_specs=specs, out_specs=out_specs, out_shape=out_shape,
        compiler_params=_params(("parallel",)), name="norm_modulate",
    )(*args)
    return out if emit_x else out[0]


def _mm_kernel(*refs, two, gate_row):
    it = iter(refs)
    x_ref = next(it)
    x2_ref = next(it) if two else None
    w_ref = next(it)
    w2_ref = next(it) if two else None
    acc = jnp.dot(x_ref[...], w_ref[...], preferred_element_type=F32)
    if two:
        acc = acc + jnp.dot(x2_ref[...], w2_ref[...], preferred_element_type=F32)
    if gate_row is not None:
        r_ref = next(it)
        m_ref = next(it)
        acc = r_ref[...] + m_ref[gate_row:gate_row + 1, :] * acc
    o_ref = next(it)
    o_ref[...] = acc.astype(o_ref.dtype)


def matmul(x, w, *, x2=None, out_dtype=F32, res=None, mod=None, gate_row=None, tm=1024, tn=512):
    m, k = x.shape
    n = w.shape[1]
    tm, tn = min(tm, m), min(tn, n)
    xspec = pl.BlockSpec((tm, k), lambda i, j: (i, 0))
    specs, args = [xspec], [x]
    if x2 is not None:
        specs += [xspec, pl.BlockSpec((k, tn), lambda i, j: (0, j)), pl.BlockSpec((k, tn), lambda i, j: (1, j))]
        args += [x2, w, w]
    else:
        specs.append(pl.BlockSpec((k, tn), lambda i, j: (0, j)))
        args.append(w)
    if res is not None:
        specs += [pl.BlockSpec((tm, tn), lambda i, j: (i, j)),
                  pl.BlockSpec((None, MOD_ROWS, tn), lambda i, j: ((i * tm) // SEG, 0, j))]
        args += [res, mod]
    return pl.pallas_call(
        functools.partial(_mm_kernel, two=x2 is not None, gate_row=gate_row if res is not None else None),
        grid=(m // tm, n // tn), in_specs=specs,
        out_specs=pl.BlockSpec((tm, tn), lambda i, j: (i, j)),
        out_shape=jax.ShapeDtypeStruct((m, n), out_dtype),
        compiler_params=_params(("parallel", "arbitrary")), name="matmul",
    )(*args)


def _extract_top16(scores, index, rank_scale):
    work = scores
    rank = jnp.full(scores.shape, NO_RANK, F32)
    vals, idxs = [], []
    for i in range(PEER_TOPK):
        m = jnp.max(work, axis=0, keepdims=True)
        first = jnp.min(jnp.where(work == m, index, NO_RANK), axis=0, keepdims=True)
        sel = index == first
        rank = jnp.where(sel, rank_scale * i, rank)
        work = jnp.where(sel, NEG_INF, work)
        vals.append(m)
        idxs.append(first)
    return vals, idxs, rank


def _stack_rows(rows, lo, n):
    r_iota = lax.broadcasted_iota(jnp.int32, (n, LANES), 0)
    out = jnp.broadcast_to(rows[lo], (n, LANES))
    for r in range(1, n):
        out = jnp.where(r_iota == r, rows[lo + r], out)
    return out


def _route_kernel(qt_ref, k1_ref, k2_ref, j1_ref, e1_ref, r2_ref, e2_ref):
    key_iota = lax.broadcasted_iota(jnp.int32, (PEER_NK, LANES), 0).astype(F32)
    j16 = lax.broadcasted_iota(jnp.int32, (PEER_TOPK, LANES), 0).astype(F32)
    j8 = lax.broadcasted_iota(jnp.int32, (SUBLANES, LANES), 0).astype(F32)
    cidx_blocks = [j16]
    for i in range(1, SUBLANES):
        cidx_blocks.append(jnp.where(j8 < PEER_TOPK // (i + 1), PEER_TOPK * i + j8, NO_RANK))
    cidx_blocks.append((j8 + SUBLANES) * PEER_TOPK)
    cand_index = jnp.concatenate(cidx_blocks, axis=0)

    def head(h, carry):
        base = pl.multiple_of(h * PEER_DQ, PEER_DQ)
        q1 = qt_ref[pl.ds(base, PEER_HALF), :].astype(BF16)
        q2 = qt_ref[pl.ds(base + PEER_HALF, PEER_HALF), :].astype(BF16)
        s1 = jnp.dot(k1_ref[h].astype(BF16), q1, preferred_element_type=F32)
        s2 = jnp.dot(k2_ref[h].astype(BF16), q2, preferred_element_type=F32)
        v1, _, rank1 = _extract_top16(s1, key_iota, float(PEER_TOPK))
        v2, _, rank2 = _extract_top16(s2, key_iota, 1.0)
        v2_lo = _stack_rows(v2, 0, SUBLANES)
        cand_blocks = [v1[0] + _stack_rows(v2, 0, PEER_TOPK)]
        for i in range(1, SUBLANES):
            cand_blocks.append(jnp.where(j8 < PEER_TOPK // (i + 1), v1[i] + v2_lo, NEG_INF))
        cand_blocks.append(_stack_rows(v1, SUBLANES, SUBLANES) + v2[0])
        sc, ci, _ = _extract_top16(jnp.concatenate(cand_blocks, axis=0), cand_index, 1.0)
        z = jnp.ones_like(sc[0])
        for i in range(1, PEER_TOPK):
            z = z + jnp.exp(sc[i] - sc[0])
        quota = jnp.zeros_like(s1)
        for c in ci:
            quota = quota + jnp.where(rank1 == PEER_TOPK * jnp.floor(c * (1.0 / PEER_TOPK)), 1.0, 0.0)
        j1_ref[h] = quota
        e1_ref[h] = jnp.exp(s1 - v1[0])
        r2_ref[h] = rank2
        e2_ref[h] = jnp.exp(s2 - v2[0]) / z
        return carry

    lax.fori_loop(0, PEER_HEADS, head, 0)


def peer_route(qt, k1, k2):
    t = qt.shape[1]
    keys = pl.BlockSpec((PEER_HEADS, PEER_NK, PEER_HALF), lambda i: (0, 0, 0))
    table = pl.BlockSpec((PEER_HEADS, PEER_NK, LANES), lambda i: (0, 0, i))
    shape = jax.ShapeDtypeStruct((PEER_HEADS, PEER_NK, t), F32)
    return pl.pallas_call(
        _route_kernel, grid=(t // LANES,),
        in_specs=[pl.BlockSpec((PEER_HEADS * PEER_DQ, LANES), lambda i: (0, i)), keys, keys],
        out_specs=[table] * 4, out_shape=[shape] * 4,
        compiler_params=_params(("parallel",)), name="peer_route",
    )(qt, k1, k2)


def _peer_kernel(ht_ref, u_ref, v_ref, j1_ref, e1_ref, r2_ref, e2_ref, o_ref, w_ref, *, n_a):
    j = pl.program_id(1)

    @pl.when(j == 0)
    def _():
        o_ref[...] = jnp.zeros_like(o_ref)
        w_ref[1] = jnp.zeros(w_ref.shape[1:], w_ref.dtype)

    st = jnp.dot(u_ref[...], ht_ref[...], preferred_element_type=F32)
    o_ref[...] += jnp.dot(w_ref[(j + 1) % 2], v_ref[...], preferred_element_type=F32)

    tile = jnp.minimum(j, pl.num_programs(1) - 2)
    first_key = (tile % (SUBLANES // n_a)) * n_a
    act = 0.5 * st * (1.0 + lax.erf(st * INV_SQRT2))
    slabs = []
    for al in range(n_a):
        row = pl.ds(first_key + al, 1)
        g = None
        for h in range(PEER_HEADS):
            term = jnp.where(r2_ref[h] < j1_ref[h, row, :], e2_ref[h], 0.0) * e1_ref[h, row, :]
            g = term if g is None else g + term
        slabs.append(g * act[al * PEER_NK:(al + 1) * PEER_NK, :])
    w_ref[j % 2] = jnp.concatenate(slabs, axis=0).T.astype(BF16)


def peer_dense(ht, u, v, j1, e1, r2, e2, *, tm=512, n_a=4):
    d, t = ht.shape
    tn = n_a * PEER_NK
    nj = u.shape[0] // tn
    per_block = SUBLANES // n_a
    once = pl.Buffered(1)
    build = lambda j: jnp.minimum(j, nj - 1)
    by_key = pl.BlockSpec((PEER_HEADS, SUBLANES, tm), lambda i, j: (0, build(j) // per_block, i))
    by_tok = pl.BlockSpec((PEER_HEADS, PEER_NK, tm), lambda i, j: (0, 0, i), pipeline_mode=once)
    return pl.pallas_call(
        functools.partial(_peer_kernel, n_a=n_a),
        grid=(t // tm, nj + 1),
        in_specs=[
            pl.BlockSpec((d, tm), lambda i, j: (0, i), pipeline_mode=once),
            pl.BlockSpec((tn, d), lambda i, j: (build(j), 0)),
            pl.BlockSpec((tn, d), lambda i, j: (jnp.maximum(j - 1, 0), 0)),
            by_key, by_key, by_tok, by_tok,
        ],
        out_specs=pl.BlockSpec((tm, d), lambda i, j: (i, 0), pipeline_mode=once),
        out_shape=jax.ShapeDtypeStruct((t, d), F32),
        scratch_shapes=[pltpu.VMEM((2, tm, tn), BF16)],
        compiler_params=_params(("parallel", "arbitrary")), name="peer_dense",
    )(ht, u, v, j1, e1, r2, e2)


HGRN_BLOCK = 256
HGRN_GROUP = 4
N_SEQ = BATCH + DEC_BATCH


def _split3(x):
    hi = x.astype(BF16)
    r = x - hi.astype(F32)
    mid = r.astype(BF16)
    return hi, mid, (r - mid.astype(F32)).astype(BF16)


def _hgrn_kernel(*refs, reverse, final):
    if final:
        q_ref, z_ref, v_ref, lb_ref, tri_ref, s0_ref, of_ref, g_ref, gn_ref, o_ref, sf_ref, st_ref = refs
    else:
        q_ref, z_ref, v_ref, lb_ref, tri_ref, s0_ref, o_ref, sf_ref, st_ref = refs
    r, p = pl.program_id(0), pl.program_id(1)
    rr = pl.num_programs(0) - 1 - r if reverse else r
    per_seq = DEC_SEQ // HGRN_BLOCK
    pos = (rr - BATCH) % per_seq
    starts = (rr < BATCH) | (pos == (per_seq - 1 if reverse else 0))

    @pl.when(starts)
    def _():
        for hh in range(HGRN_GROUP):
            st_ref[p * HGRN_GROUP + hh] = s0_ref[hh]

    q = q_ref[...]
    qs = q * jax.nn.sigmoid(q)
    z = z_ref[...]
    lb = lb_ref[...]
    logf = jnp.log(lb + (1.0 - lb) * jax.nn.sigmoid(z))
    k = (1.0 - lb) * jax.nn.sigmoid(-z)
    v = v_ref[...].astype(BF16)
    tri = tri_ref[...]
    b = sum(jnp.dot(tri, part, preferred_element_type=F32) for part in _split3(logf))

    c = HGRN_CHUNK
    t_io = lax.broadcasted_iota(jnp.int32, (c, c), 0)
    s_io = lax.broadcasted_iota(jnp.int32, (c, c), 1)
    keep = (s_io >= t_io) if reverse else (s_io <= t_io)
    mid = c // 2 - 1 if reverse else c // 2
    last = 0 if reverse else c - 1
    n_chunks = HGRN_BLOCK // c
    order = range(n_chunks - 1, -1, -1) if reverse else range(n_chunks)
    nt = (((1,), (1,)), ((), ()))
    tn = (((0,), (0,)), ((), ()))
    for hh in range(HGRN_GROUP):
        lanes = slice(hh * HGRN_DK, (hh + 1) * HGRN_DK)
        st = st_ref[p * HGRN_GROUP + hh]
        for ci in order:
            rows = slice(ci * c, (ci + 1) * c)
            bc, qc, kc, vc = b[rows, lanes], qs[rows, lanes], k[rows, lanes], v[rows, lanes]
            bm, bl = bc[mid:mid + 1], bc[last:last + 1]
            o = lax.dot_general((qc * jnp.exp(bc)).astype(BF16), st.astype(BF16), nt, preferred_element_type=F32)
            a = lax.dot_general((qc * jnp.exp(bc - bm)).astype(BF16), (kc * jnp.exp(bm - bc)).astype(BF16), nt,
                                preferred_element_type=F32)
            a = jnp.where(keep, a, 0.0).astype(BF16)
            o = o + jnp.dot(a, vc, preferred_element_type=F32)
            k_end = (kc * jnp.exp(bl - bc)).astype(BF16)
            st = st * jnp.exp(bl) + lax.dot_general(vc, k_end, tn, preferred_element_type=F32)
            if final:
                o = o + of_ref[rows, lanes]
                o = o * lax.rsqrt(jnp.mean(o * o, axis=-1, keepdims=True) + EPS) * gn_ref[...]
                gc = g_ref[rows, lanes]
                o = o * (gc * jax.nn.sigmoid(gc))
            o_ref[rows, lanes] = o.astype(o_ref.dtype)
        st_ref[p * HGRN_GROUP + hh] = st
        sf_ref[hh] = st


def _chunk_tri(reverse):
    i = jnp.arange(HGRN_BLOCK)
    same = (i[:, None] // HGRN_CHUNK) == (i[None, :] // HGRN_CHUNK)
    tri = (i[None, :] >= i[:, None]) if reverse else (i[None, :] <= i[:, None])
    return (same & tri).astype(BF16)


def _hgrn_pass(proj, z_col, lb, s0, *, reverse, o_prev=None, g_col=None, onorm=None):
    t = proj.shape[0]
    w = HGRN_GROUP * HGRN_DK
    per_group = A_W // w
    n_blk = t // HGRN_BLOCK
    blk = (lambda r: n_blk - 1 - r) if reverse else (lambda r: r)
    seq = lambda r: jnp.where(blk(r) < BATCH, blk(r), BATCH + (blk(r) - BATCH) // (DEC_SEQ // HGRN_BLOCK))
    col = lambda g: pl.BlockSpec((HGRN_BLOCK, w), lambda r, p: (blk(r), g * per_group + p))
    state = pl.BlockSpec((None, HGRN_GROUP, HGRN_DV, HGRN_DK), lambda r, p: (seq(r), p, 0, 0))
    final = o_prev is not None
    specs = [col(0), col(z_col), col(3), pl.BlockSpec((1, w), lambda r, p: (0, p)),
             pl.BlockSpec((HGRN_BLOCK, HGRN_BLOCK), lambda r, p: (0, 0)), state]
    args = [proj, proj, proj, lb, _chunk_tri(reverse), s0]
    if final:
        specs += [pl.BlockSpec((HGRN_BLOCK, w), lambda r, p: (blk(r), p)), col(g_col),
                  pl.BlockSpec((1, HGRN_DV), lambda r, p: (0, 0))]
        args += [o_prev, proj, onorm]
    return pl.pallas_call(
        functools.partial(_hgrn_kernel, reverse=reverse, final=final),
        grid=(n_blk, per_group), in_specs=specs,
        out_specs=[pl.BlockSpec((HGRN_BLOCK, w), lambda r, p: (blk(r), p)),
                   pl.BlockSpec((None, HGRN_GROUP, HGRN_DV, HGRN_DK), lambda r, p: (blk(r), p, 0, 0))],
        out_shape=[jax.ShapeDtypeStruct((t, A_W), BF16 if final else F32),
                   jax.ShapeDtypeStruct((n_blk, HGRN_H, HGRN_DV, HGRN_DK), F32)],
        scratch_shapes=[pltpu.VMEM((HGRN_H, HGRN_DV, HGRN_DK), F32)],
        compiler_params=_params(("arbitrary", "arbitrary")), name="hgrn_scan",
    )(*args)


def _gmlp_kernel(u_ref, v_ref, ws_ref, bs_ref, gn_ref, o_ref):
    for h in range(GMLP_H):
        lanes = slice(h * GMLP_CH, (h + 1) * GMLP_CH)
        vv = v_ref[:, lanes]
        vn = vv * lax.rsqrt(jnp.mean(vv * vv, axis=-1, keepdims=True) + EPS) * gn_ref[h:h + 1, :]
        sp = jnp.dot(ws_ref[h], vn.astype(BF16), preferred_element_type=F32) + bs_ref[h]
        o_ref[:, lanes] = (u_ref[:, lanes] * sp).astype(o_ref.dtype)


def gmlp_mix(proj, ws, bs, vnorm_g):
    t = proj.shape[0]
    u_col, v_col = (5 * A_W) // B_W, (5 * A_W + B_W) // B_W
    full = lambda shape: pl.BlockSpec(shape, lambda i: (0,) * len(shape))
    return pl.pallas_call(
        _gmlp_kernel, grid=(t // GMLP_CHUNK,),
        in_specs=[pl.BlockSpec((GMLP_CHUNK, B_W), lambda i: (i, u_col)),
                  pl.BlockSpec((GMLP_CHUNK, B_W), lambda i: (i, v_col)),
                  full((GMLP_H, GMLP_CHUNK, GMLP_CHUNK)), full((GMLP_H, GMLP_CHUNK, GMLP_CH)),
                  full((GMLP_H, GMLP_CH))],
        out_specs=pl.BlockSpec((GMLP_CHUNK, B_W), lambda i: (i, 0)),
        out_shape=jax.ShapeDtypeStruct((t, B_W), BF16),
        compiler_params=_params(("parallel",)), name="gmlp_mix",
    )(proj, proj, ws.astype(BF16), jnp.broadcast_to(bs[:, :, None], (GMLP_H, GMLP_CHUNK, GMLP_CH)), vnorm_g)


def _dft_matrices(n, scale):
    j = jnp.arange(n, dtype=jnp.int32)
    ang = ((j[:, None] * j[None, :]) % n).astype(F32) * (2.0 * jnp.pi / n)
    return (jnp.cos(ang) * scale).astype(BF16), (jnp.sin(ang) * scale).astype(BF16)


def _dft_ch_kernel(x_ref, w_ref, o_ref):
    o_ref[...] = jnp.dot(x_ref[...].astype(BF16), w_ref[...], preferred_element_type=F32).astype(o_ref.dtype)


def dft_channels(proj, tm=1024):
    t = proj.shape[0]
    cs, sn = _dft_matrices(FNET_CH, FNET_CH ** -0.5)
    return pl.pallas_call(
        _dft_ch_kernel, grid=(t // tm, FNET_GROUPS),
        in_specs=[pl.BlockSpec((tm, FNET_CH), lambda i, g: (i, g)),
                  pl.BlockSpec((FNET_CH, 2 * FNET_CH), lambda i, g: (0, 0))],
        out_specs=pl.BlockSpec((tm, 2 * FNET_CH), lambda i, g: (i, g)),
        out_shape=jax.ShapeDtypeStruct((t, 2 * C_W), BF16),
        compiler_params=_params(("parallel", "parallel")), name="dft_channels",
    )(proj, jnp.concatenate([cs, sn], axis=1))


def _dft_seq_kernel(*refs):
    c_ref, s_ref, p_ref, q_ref = refs[:4]
    o_ref = refs[-1]
    acc = jnp.dot(c_ref[...], p_ref[...], preferred_element_type=F32)
    acc = acc - jnp.dot(s_ref[...], q_ref[...], preferred_element_type=F32)
    o_ref[...] = acc.astype(o_ref.dtype)


def dft_sequence(pq, seq_len, n_seq, first_row, out=None, tm=1024, tn=256):
    t = pq.shape[0]
    tm = min(tm, seq_len)
    per_group = FNET_CH // tn
    cs, sn = _dft_matrices(seq_len, seq_len ** -0.5)
    base = first_row // seq_len
    pcol = lambda j: (j // per_group) * 2 * per_group + j % per_group
    specs = [pl.BlockSpec((tm, seq_len), lambda b, m, j: (m, 0)),
             pl.BlockSpec((tm, seq_len), lambda b, m, j: (m, 0)),
             pl.BlockSpec((seq_len, tn), lambda b, m, j: (base + b, pcol(j))),
             pl.BlockSpec((seq_len, tn), lambda b, m, j: (base + b, pcol(j) + per_group))]
    args = [cs, sn, pq, pq]
    aliases = {}
    if out is not None:
        specs.append(pl.BlockSpec(memory_space=pl.ANY))
        args.append(out)
        aliases = {4: 0}
    rows_per_seq = seq_len // tm
    return pl.pallas_call(
        _dft_seq_kernel, grid=(n_seq, rows_per_seq, C_W // tn), in_specs=specs,
        out_specs=pl.BlockSpec((tm, tn), lambda b, m, j: ((base + b) * rows_per_seq + m, j)),
        out_shape=jax.ShapeDtypeStruct((t, C_W), BF16), input_output_aliases=aliases,
        compiler_params=_params(("parallel", "parallel", "arbitrary")), name="dft_sequence",
    )(*args)


CONV_BLOCK = 256
CONV_HALO = 16


def _conv_kernel(a_ref, gt_ref, w_ref, cb_ref, lg_ref, lb_ref, o_ref, pad_c, pad_l, y_ref):
    r = pl.program_id(0)
    n_lanes = D_W // LANES

    def depthwise(pad, seg, n_rows, out_row):
        def chunk(ci, carry):
            lanes = pl.ds(pl.multiple_of(ci * LANES, LANES), LANES)
            acc = jnp.zeros((n_rows, LANES), F32)
            for tap in range(CONV_W):
                acc = acc + w_ref[tap:tap + 1, lanes] * pad[seg, pl.ds(CONV_HALO - CONV_PAD + tap, n_rows), lanes]
            y_ref[pl.ds(out_row, n_rows), lanes] = acc + cb_ref[:, lanes]
            return carry
        lax.fori_loop(0, n_lanes, chunk, 0)

    glu = a_ref[...] * jax.nn.sigmoid(gt_ref[...])
    zeros = jnp.zeros((CONV_HALO, D_W), F32)

    @pl.when(r < BATCH)
    def _():
        pad_c[0, 0:CONV_HALO, :] = zeros
        pad_c[0, CONV_HALO:CONV_HALO + CONV_BLOCK, :] = glu
        pad_c[0, CONV_HALO + CONV_BLOCK:, :] = zeros
        depthwise(pad_c, 0, CONV_BLOCK, 0)

    @pl.when(r >= BATCH)
    def _():
        for s in range(CONV_BLOCK // GRID_W):
            pad_l[s, 0:CONV_HALO, :] = zeros
            pad_l[s, CONV_HALO:CONV_HALO + GRID_W, :] = glu[s * GRID_W:(s + 1) * GRID_W]
            pad_l[s, CONV_HALO + GRID_W:, :] = zeros
            depthwise(pad_l, s, GRID_W, s * GRID_W)

    y = y_ref[...]
    yc = y - jnp.mean(y, axis=-1, keepdims=True)
    yn = yc * lax.rsqrt(jnp.mean(yc * yc, axis=-1, keepdims=True) + EPS) * lg_ref[...] + lb_ref[...]
    o_ref[...] = (yn * jax.nn.sigmoid(yn)).astype(o_ref.dtype)


def conv_module(proj, conv_w, conv_b, ln_g, ln_b):
    t = proj.shape[0]
    a_col, g_col = C_W // D_W, (C_W + D_W) // D_W
    row = pl.BlockSpec((1, D_W), lambda r: (0, 0))
    w_pad = jnp.pad(conv_w, ((0, 4 * SUBLANES - CONV_W), (0, 0)))
    return pl.pallas_call(
        _conv_kernel, grid=(t // CONV_BLOCK,),
        in_specs=[pl.BlockSpec((CONV_BLOCK, D_W), lambda r: (r, a_col)),
                  pl.BlockSpec((CONV_BLOCK, D_W), lambda r: (r, g_col)),
                  pl.BlockSpec((4 * SUBLANES, D_W), lambda r: (0, 0)), row, row, row],
        out_specs=pl.BlockSpec((CONV_BLOCK, D_W), lambda r: (r, 0)),
        out_shape=jax.ShapeDtypeStruct((t, D_W), BF16),
        scratch_shapes=[pltpu.VMEM((1, CONV_BLOCK + 2 * CONV_HALO, D_W), F32),
                        pltpu.VMEM((CONV_BLOCK // GRID_W, GRID_W + 2 * CONV_HALO, D_W), F32),
                        pltpu.VMEM((CONV_BLOCK, D_W), F32)],
        compiler_params=_params(("parallel",)), name="conv_module",
    )(proj, proj, w_pad, conv_b.reshape(1, D_W), ln_g.reshape(1, D_W), ln_b.reshape(1, D_W))


def kernel(x_prompt, x_sample, state_hgrn, c, c_ctx, norm_g, ada_w, ada_b, even_w_in, hgrn_lb, hgrn_onorm_g, gmlp_vnorm_g, gmlp_ws, gmlp_bs, even_w_out, odd_w_in, conv_w, conv_b, conv_ln_g, conv_ln_b, odd_w_out, peer_wq, peer_k1, peer_k2, peer_u, peer_v, final_g):
    n_ctx = BATCH * SEQ
    x = jnp.concatenate([x_prompt.reshape(n_ctx, D_MODEL), x_sample.reshape(DEC_BATCH * DEC_SEQ, D_MODEL)], axis=0)

    cond8 = jnp.zeros((8, D_MODEL), F32).at[0].set(c_ctx).at[1:1 + DEC_BATCH].set(c)
    mod_all = ada_modulation(cond8, ada_w, ada_b)
    mod_all = mod_all[:, :NSEG].reshape(DEPTH, NSEG, 6, D_MODEL)
    mod_all = jnp.pad(mod_all, ((0, 0), (0, 0), (0, MOD_ROWS - 6), (0, 0)))
    floor = jnp.cumsum(jax.nn.softmax(hgrn_lb.astype(F32), axis=0), axis=0)

    states = []
    delta = prev_mod = None
    for l in range(DEPTH):
        mod = mod_all[l]
        j = l // 2
        if delta is None:
            h = norm_modulate(x, norm_g[l, 0], mod=mod, scale_row=1, shift_row=0)
        else:
            x, h = norm_modulate(x, norm_g[l, 0], delta=delta, prev_mod=prev_mod, gate_row=5,
                                 mod=mod, scale_row=1, shift_row=0, emit_x=True)
        if l % 2 == 0:
            proj = matmul(h, even_w_in[j].astype(BF16))
            zero = jnp.zeros((BATCH, HGRN_H, HGRN_DV, HGRN_DK), F32)
            start = lambda d: jnp.concatenate([zero, jnp.swapaxes(state_hgrn[:, j, d], -1, -2)], axis=0)
            o_f, s_f = _hgrn_pass(proj, 1, floor[l, 0].reshape(1, A_W), start(0), reverse=False)
            mix_a, s_b = _hgrn_pass(proj, 2, floor[l, 1].reshape(1, A_W), start(1), reverse=True,
                                    o_prev=o_f, g_col=4, onorm=hgrn_onorm_g[j].reshape(1, HGRN_DV))
            states.append(jnp.swapaxes(jnp.stack([s_f[:BATCH], s_b[:BATCH]], axis=1), -1, -2))
            mix_b = gmlp_mix(proj, gmlp_ws[j], gmlp_bs[j], gmlp_vnorm_g[j])
            w_out = even_w_out[j]
        else:
            proj = matmul(h, odd_w_in[j].astype(BF16))
            pq = dft_channels(proj)
            mix_a = dft_sequence(pq, SEQ, BATCH, 0)
            mix_a = dft_sequence(pq, DEC_SEQ, DEC_BATCH, n_ctx, out=mix_a)
            mix_b = conv_module(proj, conv_w[j], conv_b[j], conv_ln_g[j], conv_ln_b[j])
            w_out = odd_w_out[j]
        x = matmul(mix_a, w_out.astype(BF16), x2=mix_b, res=x, mod=mod, gate_row=2)

        ht = norm_modulate(x, norm_g[l, 1], mod=mod, scale_row=4, shift_row=3, transposed=True)
        qt = matmul(peer_wq[l].T.astype(BF16), ht)
        tables = peer_route(qt, peer_k1[l], peer_k2[l])
        delta = peer_dense(ht, peer_u[l].astype(BF16), peer_v[l].astype(BF16), *tables)
        prev_mod = mod

    last = functools.partial(norm_modulate, x, final_g, delta=delta, prev_mod=prev_mod, gate_row=5, out_dtype=F32)
    y_prompt = last(rows=(0, n_ctx)).reshape(BATCH, SEQ, D_MODEL)
    y_sample = last(rows=(n_ctx, TOKENS - n_ctx)).reshape(DEC_BATCH, DEC_SEQ, D_MODEL)
    return (y_prompt, y_sample, jnp.stack(states, axis=1))
```

```python
import functools

import jax
import jax.numpy as jnp
from jax import lax
from jax.experimental import pallas as pl
from jax.experimental.pallas import tpu as pltpu

F32 = jnp.float32
BF16 = jnp.bfloat16

D_MODEL = 4096
BATCH = 16
SEQ = 256
DEPTH = 2
DEC_BATCH = 4
DEC_SEQ = 4096
GRID_W = 64
EPS = 1e-6
A_W = D_MODEL // 2
B_W = D_MODEL // 2
C_W = D_MODEL // 2
D_W = D_MODEL // 2
HGRN_DK = 128
HGRN_H = A_W // HGRN_DK
HGRN_DV = A_W // HGRN_H
HGRN_CHUNK = 32
GMLP_CHUNK = 128
GMLP_CH = 128
GMLP_H = B_W // GMLP_CH
FNET_GROUPS = 4
FNET_CH = C_W // FNET_GROUPS
CONV_W = 31
CONV_PAD = (CONV_W - 1) // 2
PEER_HEADS = 8
PEER_DQ = 256
PEER_HALF = PEER_DQ // 2
PEER_NK = 128
PEER_TOPK = 16

SEG = 4096
NSEG = 1 + DEC_BATCH
TOKENS = NSEG * SEG
MOD_ROWS = 8
LANES = 128
SUBLANES = 8

VMEM_LIMIT = 56 * 1024 * 1024
INV_SQRT2 = 0.7071067811865476
NEG_INF = float("-inf")
NO_RANK = 1e9


def _params(sem):
    return pltpu.CompilerParams(dimension_semantics=sem, vmem_limit_bytes=VMEM_LIMIT)


def _ada_kernel(c_ref, w_ref, b_ref, o_ref):
    c = c_ref[...]
    a = (c * jax.nn.sigmoid(c)).astype(BF16)
    o_ref[...] = jnp.dot(a, w_ref[...].astype(BF16), preferred_element_type=F32) + b_ref[...]


def ada_modulation(cond8, ada_w, ada_b, tn=512):
    depth, d, n = ada_w.shape
    return pl.pallas_call(
        _ada_kernel,
        grid=(depth, n // tn),
        in_specs=[
            pl.BlockSpec((8, d), lambda l, j: (0, 0)),
            pl.BlockSpec((None, d, tn), lambda l, j: (l, 0, j)),
            pl.BlockSpec((None, 1, tn), lambda l, j: (l, 0, j)),
        ],
        out_specs=pl.BlockSpec((None, 8, tn), lambda l, j: (l, 0, j)),
        out_shape=jax.ShapeDtypeStruct((depth, 8, n), F32),
        compiler_params=_params(("parallel", "parallel")),
        name="ada_modulation",
    )(cond8, ada_w, ada_b.reshape(depth, 1, n))


def _norm_kernel(*refs, gate_row, scale_row, shift_row, emit_x, transposed):
    it = iter(refs)
    x_ref = next(it)
    x = x_ref[...]
    if gate_row is not None:
        d_ref = next(it)
        pm_ref = next(it)
        x = x + pm_ref[gate_row:gate_row + 1, :] * d_ref[...]
    g_ref = next(it)
    m_ref = next(it) if scale_row is not None else None
    xo_ref = next(it) if emit_x else None
    h_ref = next(it)
    if emit_x:
        xo_ref[...] = x
    ms = jnp.mean(x * x, axis=-1, keepdims=True)
    y = x * lax.rsqrt(ms + EPS) * g_ref[...]
    if scale_row is not None:
        y = y * (1.0 + m_ref[scale_row:scale_row + 1, :]) + m_ref[shift_row:shift_row + 1, :]
    if transposed:
        y = y.T
    h_ref[...] = y.astype(h_ref.dtype)


def norm_modulate(x, g, *, delta=None, prev_mod=None, gate_row=None, mod=None, scale_row=None,
                  shift_row=None, emit_x=False, transposed=False, out_dtype=BF16, rows=None, tr=256):
    d = x.shape[1]
    first, t = rows if rows is not None else (0, x.shape[0])
    blk0 = first // tr
    row = pl.BlockSpec((tr, d), lambda i: (i, 0))
    in_row = pl.BlockSpec((tr, d), lambda i: (blk0 + i, 0))
    modspec = pl.BlockSpec((None, MOD_ROWS, d), lambda i: (((blk0 + i) * tr) // SEG, 0, 0))
    args, specs = [x], [in_row]
    if delta is not None:
        args += [delta, prev_mod]
        specs += [in_row, modspec]
    args.append(g.reshape(1, d))
    specs.append(pl.BlockSpec((1, d), lambda i: (0, 0)))
    if mod is not None:
        args.append(mod)
        specs.append(modspec)
    if transposed:
        out_shape = [jax.ShapeDtypeStruct((d, t), out_dtype)]
        out_specs = [pl.BlockSpec((d, tr), lambda i: (0, i))]
    else:
        out_shape = [jax.ShapeDtypeStruct((t, d), out_dtype)]
        out_specs = [row]
    if emit_x:
        out_shape.insert(0, jax.ShapeDtypeStruct((t, d), F32))
        out_specs.insert(0, row)
    kern = functools.partial(_norm_kernel, gate_row=gate_row if delta is not None else None,
                             scale_row=scale_row if mod is not None else None,
                             shift_row=shift_row, emit_x=emit_x, transposed=transposed)
    out = pl.pallas_call(
        kern, grid=(t // tr,), in_specs=specs, out_specs=out_specs, out_shape=out_shape,
        compiler_params=_params(("parallel",)), name="norm_modulate",
    )(*args)
    return out if emit_x else out[0]


def _cast_kernel(x_ref, o_ref):
    o_ref[...] = x_ref[...].astype(o_ref.dtype)


CAST_BLOCK_ELEMS = 2 ** 21


def to_bf16(w, layer):
    _, m, n = w.shape
    rows = 1 << ((CAST_BLOCK_ELEMS // n).bit_length() - 1)
    assert m % rows == 0 and rows % (2 * SUBLANES) == 0
    return pl.pallas_call(
        _cast_kernel, grid=(m // rows,),
        in_specs=[pl.BlockSpec((None, rows, n), lambda i: (layer, i, 0))],
        out_specs=pl.BlockSpec((rows, n), lambda i: (i, 0)),
        out_shape=jax.ShapeDtypeStruct((m, n), BF16),
        compiler_params=_params(("parallel",)), name="to_bf16",
    )(w)


def _mm_kernel(*refs, two, gate_row):
    it = iter(refs)
    x_ref = next(it)
    x2_ref = next(it) if two else None
    w_ref = next(it)
    w2_ref = next(it) if two else None
    acc = jnp.dot(x_ref[...], w_ref[...], preferred_element_type=F32)
    if two:
        acc = acc + jnp.dot(x2_ref[...], w2_ref[...], preferred_element_type=F32)
    if gate_row is not None:
        r_ref = next(it)
        m_ref = next(it)
        acc = r_ref[...] + m_ref[gate_row:gate_row + 1, :] * acc
    o_ref = next(it)
    o_ref[...] = acc.astype(o_ref.dtype)


def matmul(x, w, *, x2=None, out_dtype=F32, res=None, mod=None, gate_row=None, tm=1024, tn=1024):
    m, k = x.shape
    n = w.shape[1]
    tm, tn = min(tm, m), min(tn, n)
    xspec = pl.BlockSpec((tm, k), lambda i, j: (i, 0))
    specs, args = [xspec], [x]
    if x2 is not None:
        specs += [xspec, pl.BlockSpec((k, tn), lambda i, j: (0, j)), pl.BlockSpec((k, tn), lambda i, j: (1, j))]
        args += [x2, w, w]
    else:
        specs.append(pl.BlockSpec((k, tn), lambda i, j: (0, j)))
        args.append(w)
    if res is not None:
        specs += [pl.BlockSpec((tm, tn), lambda i, j: (i, j)),
                  pl.BlockSpec((None, MOD_ROWS, tn), lambda i, j: ((i * tm) // SEG, 0, j))]
        args += [res, mod]
    return pl.pallas_call(
        functools.partial(_mm_kernel, two=x2 is not None, gate_row=gate_row if res is not None else None),
        grid=(m // tm, n // tn), in_specs=specs,
        out_specs=pl.BlockSpec((tm, tn), lambda i, j: (i, j)),
        out_shape=jax.ShapeDtypeStruct((m, n), out_dtype),
        compiler_params=_params(("parallel", "arbitrary")), name="matmul",
    )(*args)


ROUTE_TOKENS = 512


def _extract_top16(scores, index, rank_scale):
    work = scores
    rank = jnp.full(scores.shape, NO_RANK, F32)
    vals, idxs = [], []
    for i in range(PEER_TOPK):
        m = jnp.max(work, axis=0, keepdims=True)
        first = jnp.min(jnp.where(work == m, index, NO_RANK), axis=0, keepdims=True)
        sel = index == first
        rank = jnp.where(sel, rank_scale * i, rank)
        work = jnp.where(sel, NEG_INF, work)
        vals.append(m)
        idxs.append(first)
    return vals, idxs, rank


def _stack_rows(rows, lo, n):
    r_iota = lax.broadcasted_iota(jnp.int32, (n, ROUTE_TOKENS), 0)
    out = jnp.broadcast_to(rows[lo], (n, ROUTE_TOKENS))
    for r in range(1, n):
        out = jnp.where(r_iota == r, rows[lo + r], out)
    return out


def _route_kernel(qt_ref, k1_ref, k2_ref, j1_ref, e1_ref, r2_ref, e2_ref):
    key_iota = lax.broadcasted_iota(jnp.int32, (PEER_NK, ROUTE_TOKENS), 0).astype(F32)
    j16 = lax.broadcasted_iota(jnp.int32, (PEER_TOPK, ROUTE_TOKENS), 0).astype(F32)
    j8 = lax.broadcasted_iota(jnp.int32, (SUBLANES, ROUTE_TOKENS), 0).astype(F32)
    cidx_blocks = [j16]
    for i in range(1, SUBLANES):
        cidx_blocks.append(jnp.where(j8 < PEER_TOPK // (i + 1), PEER_TOPK * i + j8, NO_RANK))
    cidx_blocks.append((j8 + SUBLANES) * PEER_TOPK)
    cand_index = jnp.concatenate(cidx_blocks, axis=0)

    def head(h, carry):
        base = pl.multiple_of(h * PEER_DQ, PEER_DQ)
        q1 = qt_ref[pl.ds(base, PEER_HALF), :].astype(BF16)
        q2 = qt_ref[pl.ds(base + PEER_HALF, PEER_HALF), :].astype(BF16)
        s1 = jnp.dot(k1_ref[h].astype(BF16), q1, preferred_element_type=F32)
        s2 = jnp.dot(k2_ref[h].astype(BF16), q2, preferred_element_type=F32)
        v1, _, rank1 = _extract_top16(s1, key_iota, float(PEER_TOPK))
        v2, _, rank2 = _extract_top16(s2, key_iota, 1.0)
        v2_lo = _stack_rows(v2, 0, SUBLANES)
        cand_blocks = [v1[0] + _stack_rows(v2, 0, PEER_TOPK)]
        for i in range(1, SUBLANES):
            cand_blocks.append(jnp.where(j8 < PEER_TOPK // (i + 1), v1[i] + v2_lo, NEG_INF))
        cand_blocks.append(_stack_rows(v1, SUBLANES, SUBLANES) + v2[0])
        sc, ci, _ = _extract_top16(jnp.concatenate(cand_blocks, axis=0), cand_index, 1.0)
        z = jnp.ones_like(sc[0])
        for i in range(1, PEER_TOPK):
            z = z + jnp.exp(sc[i] - sc[0])
        quota = jnp.zeros_like(s1)
        for c in ci:
            quota = quota + jnp.where(rank1 == PEER_TOPK * jnp.floor(c * (1.0 / PEER_TOPK)), 1.0, 0.0)
        j1_ref[h] = quota
        e1_ref[h] = jnp.exp(s1 - v1[0])
        r2_ref[h] = rank2
        e2_ref[h] = jnp.exp(s2 - v2[0]) / z
        return carry

    lax.fori_loop(0, PEER_HEADS, head, 0)


def peer_route(qt, k1, k2):
    t = qt.shape[1]
    keys = pl.BlockSpec((PEER_HEADS, PEER_NK, PEER_HALF), lambda i: (0, 0, 0))
    table = pl.BlockSpec((PEER_HEADS, PEER_NK, ROUTE_TOKENS), lambda i: (0, 0, i))
    shape = jax.ShapeDtypeStruct((PEER_HEADS, PEER_NK, t), F32)
    return pl.pallas_call(
        _route_kernel, grid=(t // ROUTE_TOKENS,),
        in_specs=[pl.BlockSpec((PEER_HEADS * PEER_DQ, ROUTE_TOKENS), lambda i: (0, i)), keys, keys],
        out_specs=[table] * 4, out_shape=[shape] * 4,
        compiler_params=_params(("parallel",)), name="peer_route",
    )(qt, k1, k2)


def _peer_kernel(ht_ref, u_ref, v_ref, j1_ref, e1_ref, r2_ref, e2_ref, o_ref, w_ref, *, n_a):
    j = pl.program_id(1)

    @pl.when(j == 0)
    def _():
        o_ref[...] = jnp.zeros_like(o_ref)
        w_ref[1] = jnp.zeros(w_ref.shape[1:], w_ref.dtype)

    st = jnp.dot(u_ref[...], ht_ref[...], preferred_element_type=F32)
    o_ref[...] += jnp.dot(w_ref[(j + 1) % 2], v_ref[...], preferred_element_type=F32)

    tile = jnp.minimum(j, pl.num_programs(1) - 2)
    first_key = (tile % (SUBLANES // n_a)) * n_a
    act = 0.5 * st * (1.0 + lax.erf(st * INV_SQRT2))
    slabs = []
    for al in range(n_a):
        row = pl.ds(first_key + al, 1)
        g = None
        for h in range(PEER_HEADS):
            term = jnp.where(r2_ref[h] < j1_ref[h, row, :], e2_ref[h], 0.0) * e1_ref[h, row, :]
            g = term if g is None else g + term
        slabs.append(g * act[al * PEER_NK:(al + 1) * PEER_NK, :])
    w_ref[j % 2] = jnp.concatenate(slabs, axis=0).T.astype(BF16)


def peer_dense(ht, u, v, j1, e1, r2, e2, *, tm=512, n_a=8):
    d, t = ht.shape
    tn = n_a * PEER_NK
    nj = u.shape[0] // tn
    per_block = SUBLANES // n_a
    once = pl.Buffered(1)
    build = lambda j: jnp.minimum(j, nj - 1)
    by_key = pl.BlockSpec((PEER_HEADS, SUBLANES, tm), lambda i, j: (0, build(j) // per_block, i))
    by_tok = pl.BlockSpec((PEER_HEADS, PEER_NK, tm), lambda i, j: (0, 0, i), pipeline_mode=once)
    return pl.pallas_call(
        functools.partial(_peer_kernel, n_a=n_a),
        grid=(t // tm, nj + 1),
        in_specs=[
            pl.BlockSpec((d, tm), lambda i, j: (0, i), pipeline_mode=once),
            pl.BlockSpec((tn, d), lambda i, j: (build(j), 0)),
            pl.BlockSpec((tn, d), lambda i, j: (jnp.maximum(j - 1, 0), 0)),
            by_key, by_key, by_tok, by_tok,
        ],
        out_specs=pl.BlockSpec((tm, d), lambda i, j: (i, 0), pipeline_mode=once),
        out_shape=jax.ShapeDtypeStruct((t, d), F32),
        scratch_shapes=[pltpu.VMEM((2, tm, tn), BF16)],
        compiler_params=_params(("parallel", "arbitrary")), name="peer_dense",
    )(ht, u, v, j1, e1, r2, e2)


HGRN_BLOCK = 256
HGRN_GROUP = 4
N_SEQ = BATCH + DEC_BATCH


def _split3(x):
    hi = x.astype(BF16)
    r = x - hi.astype(F32)
    mid = r.astype(BF16)
    return hi, mid, (r - mid.astype(F32)).astype(BF16)


def _hgrn_kernel(*refs, reverse, final):
    if final:
        q_ref, z_ref, v_ref, lb_ref, tri_ref, s0_ref, of_ref, g_ref, gn_ref, o_ref, sf_ref, st_ref = refs
    else:
        q_ref, z_ref, v_ref, lb_ref, tri_ref, s0_ref, o_ref, sf_ref, st_ref = refs
    r, p = pl.program_id(0), pl.program_id(1)
    rr = pl.num_programs(0) - 1 - r if reverse else r
    per_seq = DEC_SEQ // HGRN_BLOCK
    pos = (rr - BATCH) % per_seq
    starts = (rr < BATCH) | (pos == (per_seq - 1 if reverse else 0))

    @pl.when(starts)
    def _():
        for hh in range(HGRN_GROUP):
            st_ref[p * HGRN_GROUP + hh] = s0_ref[hh]

    q = q_ref[...]
    qs = q * jax.nn.sigmoid(q)
    z = z_ref[...]
    lb = lb_ref[...]
    logf = jnp.log(lb + (1.0 - lb) * jax.nn.sigmoid(z))
    k = (1.0 - lb) * jax.nn.sigmoid(-z)
    v = v_ref[...].astype(BF16)
    tri = tri_ref[...]
    b = sum(jnp.dot(tri, part, preferred_element_type=F32) for part in _split3(logf))

    c = HGRN_CHUNK
    t_io = lax.broadcasted_iota(jnp.int32, (c, c), 0)
    s_io = lax.broadcasted_iota(jnp.int32, (c, c), 1)
    keep = (s_io >= t_io) if reverse else (s_io <= t_io)
    mid = c // 2 - 1 if reverse else c // 2
    last = 0 if reverse else c - 1
    n_chunks = HGRN_BLOCK // c
    order = range(n_chunks - 1, -1, -1) if reverse else range(n_chunks)
    nt = (((1,), (1,)), ((), ()))
    tn = (((0,), (0,)), ((), ()))
    for hh in range(HGRN_GROUP):
        lanes = slice(hh * HGRN_DK, (hh + 1) * HGRN_DK)
        st = st_ref[p * HGRN_GROUP + hh]
        for ci in order:
            rows = slice(ci * c, (ci + 1) * c)
            bc, qc, kc, vc = b[rows, lanes], qs[rows, lanes], k[rows, lanes], v[rows, lanes]
            bm, bl = bc[mid:mid + 1], bc[last:last + 1]
            o = lax.dot_general((qc * jnp.exp(bc)).astype(BF16), st.astype(BF16), nt, preferred_element_type=F32)
            a = lax.dot_general((qc * jnp.exp(bc - bm)).astype(BF16), (kc * jnp.exp(bm - bc)).astype(BF16), nt,
                                preferred_element_type=F32)
            a = jnp.where(keep, a, 0.0).astype(BF16)
            o = o + jnp.dot(a, vc, preferred_element_type=F32)
            k_end = (kc * jnp.exp(bl - bc)).astype(BF16)
            st = st * jnp.exp(bl) + lax.dot_general(vc, k_end, tn, preferred_element_type=F32)
            if final:
                o = o + of_ref[rows, lanes]
                o = o * lax.rsqrt(jnp.mean(o * o, axis=-1, keepdims=True) + EPS) * gn_ref[...]
                gc = g_ref[rows, lanes]
                o = o * (gc * jax.nn.sigmoid(gc))
            o_ref[rows, lanes] = o.astype(o_ref.dtype)
        st_ref[p * HGRN_GROUP + hh] = st
        sf_ref[hh] = st


def _chunk_tri(reverse):
    i = jnp.arange(HGRN_BLOCK)
    same = (i[:, None] // HGRN_CHUNK) == (i[None, :] // HGRN_CHUNK)
    tri = (i[None, :] >= i[:, None]) if reverse else (i[None, :] <= i[:, None])
    return (same & tri).astype(BF16)


def _hgrn_pass(proj, z_col, lb, s0, *, reverse, o_prev=None, g_col=None, onorm=None):
    t = proj.shape[0]
    w = HGRN_GROUP * HGRN_DK
    per_group = A_W // w
    n_blk = t // HGRN_BLOCK
    blk = (lambda r: n_blk - 1 - r) if reverse else (lambda r: r)
    seq = lambda r: jnp.where(blk(r) < BATCH, blk(r), BATCH + (blk(r) - BATCH) // (DEC_SEQ // HGRN_BLOCK))
    col = lambda g: pl.BlockSpec((HGRN_BLOCK, w), lambda r, p: (blk(r), g * per_group + p))
    state = pl.BlockSpec((None, HGRN_GROUP, HGRN_DV, HGRN_DK), lambda r, p: (seq(r), p, 0, 0))
    final = o_prev is not None
    specs = [col(0), col(z_col), col(3), pl.BlockSpec((1, w), lambda r, p: (0, p)),
             pl.BlockSpec((HGRN_BLOCK, HGRN_BLOCK), lambda r, p: (0, 0)), state]
    args = [proj, proj, proj, lb, _chunk_tri(reverse), s0]
    if final:
        specs += [pl.BlockSpec((HGRN_BLOCK, w), lambda r, p: (blk(r), p)), col(g_col),
                  pl.BlockSpec((1, HGRN_DV), lambda r, p: (0, 0))]
        args += [o_prev, proj, onorm]
    return pl.pallas_call(
        functools.partial(_hgrn_kernel, reverse=reverse, final=final),
        grid=(n_blk, per_group), in_specs=specs,
        out_specs=[pl.BlockSpec((HGRN_BLOCK, w), lambda r, p: (blk(r), p)),
                   pl.BlockSpec((None, HGRN_GROUP, HGRN_DV, HGRN_DK), lambda r, p: (blk(r), p, 0, 0))],
        out_shape=[jax.ShapeDtypeStruct((t, A_W), BF16 if final else F32),
                   jax.ShapeDtypeStruct((n_blk, HGRN_H, HGRN_DV, HGRN_DK), F32)],
        scratch_shapes=[pltpu.VMEM((HGRN_H, HGRN_DV, HGRN_DK), F32)],
        compiler_params=_params(("arbitrary", "arbitrary")), name="hgrn_scan",
    )(*args)


def _gmlp_kernel(u_ref, v_ref, ws_ref, bs_ref, gn_ref, o_ref):
    for h in range(GMLP_H):
        lanes = slice(h * GMLP_CH, (h + 1) * GMLP_CH)
        vv = v_ref[:, lanes]
        vn = vv * lax.rsqrt(jnp.mean(vv * vv, axis=-1, keepdims=True) + EPS) * gn_ref[h:h + 1, :]
        sp = jnp.dot(ws_ref[h], vn.astype(BF16), preferred_element_type=F32) + bs_ref[h]
        o_ref[:, lanes] = (u_ref[:, lanes] * sp).astype(o_ref.dtype)


def gmlp_mix(proj, ws, bs, vnorm_g):
    t = proj.shape[0]
    u_col, v_col = (5 * A_W) // B_W, (5 * A_W + B_W) // B_W
    full = lambda shape: pl.BlockSpec(shape, lambda i: (0,) * len(shape))
    return pl.pallas_call(
        _gmlp_kernel, grid=(t // GMLP_CHUNK,),
        in_specs=[pl.BlockSpec((GMLP_CHUNK, B_W), lambda i: (i, u_col)),
                  pl.BlockSpec((GMLP_CHUNK, B_W), lambda i: (i, v_col)),
                  full((GMLP_H, GMLP_CHUNK, GMLP_CHUNK)), full((GMLP_H, GMLP_CHUNK, GMLP_CH)),
                  full((GMLP_H, GMLP_CH))],
        out_specs=pl.BlockSpec((GMLP_CHUNK, B_W), lambda i: (i, 0)),
        out_shape=jax.ShapeDtypeStruct((t, B_W), BF16),
        compiler_params=_params(("parallel",)), name="gmlp_mix",
    )(proj, proj, ws.astype(BF16), jnp.broadcast_to(bs[:, :, None], (GMLP_H, GMLP_CHUNK, GMLP_CH)), vnorm_g)


DFT_SPLIT = 64


def _dft_matrices(n, scale):
    j = jnp.arange(n, dtype=jnp.int32)[:, None]
    unit = 2.0 * jnp.pi / n
    a_hi = ((j * (DFT_SPLIT * jnp.arange(n // DFT_SPLIT, dtype=jnp.int32)[None, :])) % n).astype(F32) * unit
    a_lo = ((j * jnp.arange(DFT_SPLIT, dtype=jnp.int32)[None, :]) % n).astype(F32) * unit
    ch, sh = jnp.cos(a_hi)[:, :, None], jnp.sin(a_hi)[:, :, None]
    cl, sl = (jnp.cos(a_lo) * scale)[:, None, :], (jnp.sin(a_lo) * scale)[:, None, :]
    return ((ch * cl - sh * sl).reshape(n, n).astype(BF16), (sh * cl + ch * sl).reshape(n, n).astype(BF16))


def _dft_ch_kernel(x_ref, w_ref, o_ref):
    o_ref[...] = jnp.dot(x_ref[...].astype(BF16), w_ref[...], preferred_element_type=F32).astype(o_ref.dtype)


def dft_channels(proj, tm=1024):
    t = proj.shape[0]
    cs, sn = _dft_matrices(FNET_CH, FNET_CH ** -0.5)
    return pl.pallas_call(
        _dft_ch_kernel, grid=(t // tm, FNET_GROUPS),
        in_specs=[pl.BlockSpec((tm, FNET_CH), lambda i, g: (i, g)),
                  pl.BlockSpec((FNET_CH, 2 * FNET_CH), lambda i, g: (0, 0))],
        out_specs=pl.BlockSpec((tm, 2 * FNET_CH), lambda i, g: (i, g)),
        out_shape=jax.ShapeDtypeStruct((t, 2 * C_W), BF16),
        compiler_params=_params(("parallel", "parallel")), name="dft_channels",
    )(proj, jnp.concatenate([cs, sn], axis=1))


def _dft_seq_kernel(*refs):
    c_ref, s_ref, p_ref, q_ref = refs[:4]
    o_ref = refs[-1]
    acc = jnp.dot(c_ref[...], p_ref[...], preferred_element_type=F32)
    acc = acc - jnp.dot(s_ref[...], q_ref[...], preferred_element_type=F32)
    o_ref[...] = acc.astype(o_ref.dtype)


def dft_sequence(pq, seq_len, n_seq, first_row, out=None, tm=1024, tn=256):
    t = pq.shape[0]
    tm = min(tm, seq_len)
    per_group = FNET_CH // tn
    cs, sn = _dft_matrices(seq_len, seq_len ** -0.5)
    base = first_row // seq_len
    pcol = lambda j: (j // per_group) * 2 * per_group + j % per_group
    specs = [pl.BlockSpec((tm, seq_len), lambda b, m, j: (m, 0)),
             pl.BlockSpec((tm, seq_len), lambda b, m, j: (m, 0)),
             pl.BlockSpec((seq_len, tn), lambda b, m, j: (base + b, pcol(j))),
             pl.BlockSpec((seq_len, tn), lambda b, m, j: (base + b, pcol(j) + per_group))]
    args = [cs, sn, pq, pq]
    aliases = {}
    if out is not None:
        specs.append(pl.BlockSpec(memory_space=pl.ANY))
        args.append(out)
        aliases = {4: 0}
    rows_per_seq = seq_len // tm
    return pl.pallas_call(
        _dft_seq_kernel, grid=(n_seq, rows_per_seq, C_W // tn), in_specs=specs,
        out_specs=pl.BlockSpec((tm, tn), lambda b, m, j: ((base + b) * rows_per_seq + m, j)),
        out_shape=jax.ShapeDtypeStruct((t, C_W), BF16), input_output_aliases=aliases,
        compiler_params=_params(("parallel", "parallel", "arbitrary")), name="dft_sequence",
    )(*args)


CONV_BLOCK = 256
CONV_HALO = 16


def _conv_kernel(a_ref, gt_ref, w_ref, cb_ref, lg_ref, lb_ref, o_ref, pad_c, pad_l, y_ref):
    r = pl.program_id(0)
    n_lanes = D_W // LANES

    def depthwise(pad, seg, n_rows, out_row):
        def chunk(ci, carry):
            lanes = pl.ds(pl.multiple_of(ci * LANES, LANES), LANES)
            acc = jnp.zeros((n_rows, LANES), F32)
            for tap in range(CONV_W):
                acc = acc + w_ref[tap:tap + 1, lanes] * pad[seg, pl.ds(CONV_HALO - CONV_PAD + tap, n_rows), lanes]
            y_ref[pl.ds(out_row, n_rows), lanes] = acc + cb_ref[:, lanes]
            return carry
        lax.fori_loop(0, n_lanes, chunk, 0)

    glu = a_ref[...] * jax.nn.sigmoid(gt_ref[...])
    zeros = jnp.zeros((CONV_HALO, D_W), F32)

    @pl.when(r < BATCH)
    def _():
        pad_c[0, 0:CONV_HALO, :] = zeros
        pad_c[0, CONV_HALO:CONV_HALO + CONV_BLOCK, :] = glu
        pad_c[0, CONV_HALO + CONV_BLOCK:, :] = zeros
        depthwise(pad_c, 0, CONV_BLOCK, 0)

    @pl.when(r >= BATCH)
    def _():
        for s in range(CONV_BLOCK // GRID_W):
            pad_l[s, 0:CONV_HALO, :] = zeros
            pad_l[s, CONV_HALO:CONV_HALO + GRID_W, :] = glu[s * GRID_W:(s + 1) * GRID_W]
            pad_l[s, CONV_HALO + GRID_W:, :] = zeros
            depthwise(pad_l, s, GRID_W, s * GRID_W)

    y = y_ref[...]
    yc = y - jnp.mean(y, axis=-1, keepdims=True)
    yn = yc * lax.rsqrt(jnp.mean(yc * yc, axis=-1, keepdims=True) + EPS) * lg_ref[...] + lb_ref[...]
    o_ref[...] = (yn * jax.nn.sigmoid(yn)).astype(o_ref.dtype)


def conv_module(proj, conv_w, conv_b, ln_g, ln_b):
    t = proj.shape[0]
    a_col, g_col = C_W // D_W, (C_W + D_W) // D_W
    row = pl.BlockSpec((1, D_W), lambda r: (0, 0))
    w_pad = jnp.pad(conv_w, ((0, 4 * SUBLANES - CONV_W), (0, 0)))
    return pl.pallas_call(
        _conv_kernel, grid=(t // CONV_BLOCK,),
        in_specs=[pl.BlockSpec((CONV_BLOCK, D_W), lambda r: (r, a_col)),
                  pl.BlockSpec((CONV_BLOCK, D_W), lambda r: (r, g_col)),
                  pl.BlockSpec((4 * SUBLANES, D_W), lambda r: (0, 0)), row, row, row],
        out_specs=pl.BlockSpec((CONV_BLOCK, D_W), lambda r: (r, 0)),
        out_shape=jax.ShapeDtypeStruct((t, D_W), BF16),
        scratch_shapes=[pltpu.VMEM((1, CONV_BLOCK + 2 * CONV_HALO, D_W), F32),
                        pltpu.VMEM((CONV_BLOCK // GRID_W, GRID_W + 2 * CONV_HALO, D_W), F32),
                        pltpu.VMEM((CONV_BLOCK, D_W), F32)],
        compiler_params=_params(("parallel",)), name="conv_module",
    )(proj, proj, w_pad, conv_b.reshape(1, D_W), ln_g.reshape(1, D_W), ln_b.reshape(1, D_W))


def kernel(x_prompt, x_sample, state_hgrn, c, c_ctx, norm_g, ada_w, ada_b, even_w_in, hgrn_lb, hgrn_onorm_g, gmlp_vnorm_g, gmlp_ws, gmlp_bs, even_w_out, odd_w_in, conv_w, conv_b, conv_ln_g, conv_ln_b, odd_w_out, peer_wq, peer_k1, peer_k2, peer_u, peer_v, final_g):
    n_ctx = BATCH * SEQ
    x = jnp.concatenate([x_prompt.reshape(n_ctx, D_MODEL), x_sample.reshape(DEC_BATCH * DEC_SEQ, D_MODEL)], axis=0)

    cond8 = jnp.zeros((8, D_MODEL), F32).at[0].set(c_ctx).at[1:1 + DEC_BATCH].set(c)
    mod_all = ada_modulation(cond8, ada_w, ada_b)
    mod_all = mod_all[:, :NSEG].reshape(DEPTH, NSEG, 6, D_MODEL)
    mod_all = jnp.pad(mod_all, ((0, 0), (0, 0), (0, MOD_ROWS - 6), (0, 0)))
    floor = jnp.cumsum(jax.nn.softmax(hgrn_lb.astype(F32), axis=0), axis=0)

    states = []
    delta = prev_mod = None
    for l in range(DEPTH):
        mod = mod_all[l]
        j = l // 2
        if delta is None:
            h = norm_modulate(x, norm_g[l, 0], mod=mod, scale_row=1, shift_row=0)
        else:
            x, h = norm_modulate(x, norm_g[l, 0], delta=delta, prev_mod=prev_mod, gate_row=5,
                                 mod=mod, scale_row=1, shift_row=0, emit_x=True)
        if l % 2 == 0:
            proj = matmul(h, to_bf16(even_w_in, j))
            zero = jnp.zeros((BATCH, HGRN_H, HGRN_DV, HGRN_DK), F32)
            start = lambda d: jnp.concatenate([zero, jnp.swapaxes(state_hgrn[:, j, d], -1, -2)], axis=0)
            o_f, s_f = _hgrn_pass(proj, 1, floor[l, 0].reshape(1, A_W), start(0), reverse=False)
            mix_a, s_b = _hgrn_pass(proj, 2, floor[l, 1].reshape(1, A_W), start(1), reverse=True,
                                    o_prev=o_f, g_col=4, onorm=hgrn_onorm_g[j].reshape(1, HGRN_DV))
            states.append(jnp.swapaxes(jnp.stack([s_f[:BATCH], s_b[:BATCH]], axis=1), -1, -2))
            mix_b = gmlp_mix(proj, gmlp_ws[j], gmlp_bs[j], gmlp_vnorm_g[j])
            w_out = to_bf16(even_w_out, j)
        else:
            proj = matmul(h, to_bf16(odd_w_in, j))
            pq = dft_channels(proj)
            mix_a = dft_sequence(pq, SEQ, BATCH, 0)
            mix_a = dft_sequence(pq, DEC_SEQ, DEC_BATCH, n_ctx, out=mix_a)
            mix_b = conv_module(proj, conv_w[j], conv_b[j], conv_ln_g[j], conv_ln_b[j])
            w_out = to_bf16(odd_w_out, j)
        x = matmul(mix_a, w_out, x2=mix_b, res=x, mod=mod, gate_row=2)

        ht = norm_modulate(x, norm_g[l, 1], mod=mod, scale_row=4, shift_row=3, transposed=True)
        qt = matmul(peer_wq[l].T.astype(BF16), ht)
        tables = peer_route(qt, peer_k1[l], peer_k2[l])
        delta = peer_dense(ht, to_bf16(peer_u, l), to_bf16(peer_v, l), *tables)
        prev_mod = mod

    last = functools.partial(norm_modulate, x, final_g, delta=delta, prev_mod=prev_mod, gate_row=5, out_dtype=F32)
    y_prompt = last(rows=(0, n_ctx)).reshape(BATCH, SEQ, D_MODEL)
    y_sample = last(rows=(n_ctx, TOKENS - n_ctx)).reshape(DEC_BATCH, DEC_SEQ, D_MODEL)
    return (y_prompt, y_sample, jnp.stack(states, axis=1))
```
